```python
import jax
import jax.numpy as jnp
from jax import lax
import numpy as np

D_MODEL = 2048
BATCH = 16
SEQ = 256
DEPTH = 4
DEC_BATCH = 8
DEC_SEQ = 4096
PAST_LEN = 256

GRID_W = 64
EPS = 1e-6
N_MOD = 6
A_HEADS = 4
A_DK = 128
A_DV = 128
A_WIDTH = A_HEADS * A_DK
A_VWIDTH = A_HEADS * A_DV
SCAN_CHUNK = 64
B_HEADS = 8
B_KV_HEADS = 2
HEAD_DIM = 128
B_WIDTH = B_HEADS * HEAD_DIM
B_KV_WIDTH = B_KV_HEADS * HEAD_DIM
Q_BLOCK = 128
ROPE_THETA = 10000.0
C_GROUPS = 4
C_GDIM = 128
C_WIDTH = C_GROUPS * C_GDIM
MIX_CHUNK = 128
N_BRANCH = 3
IN_SIZES = (A_WIDTH, A_WIDTH, A_WIDTH, A_VWIDTH, A_VWIDTH, B_WIDTH, B_KV_WIDTH, B_KV_WIDTH, C_WIDTH, C_WIDTH, N_BRANCH * D_MODEL)
N_IN = 3 * A_WIDTH + 2 * A_VWIDTH + B_WIDTH + 2 * B_KV_WIDTH + 2 * C_WIDTH + N_BRANCH * D_MODEL
N_EXPERTS = 16
N_GROUPS = 4
EXPERTS_PER_GROUP = N_EXPERTS // N_GROUPS
TOP_K = 2
EXPERT_FF = 1408
MOE_BLOCK = 256

kernel_name = "hybrid_flow_backbone_ctx_prefix_step"


def rms_norm(x, g):
    xf = x.astype(jnp.float32)
    y = xf * lax.rsqrt(jnp.mean(xf * xf, axis=-1, keepdims=True) + EPS)
    return (y * g.astype(jnp.float32)).astype(x.dtype)


def ada_params(cond, w_mod_l, b_mod_l):
    m = jax.nn.silu(cond) @ w_mod_l + b_mod_l
    return jnp.split(m[:, None, :], N_MOD, axis=-1)


def modulate(x, g, shift, scale):
    return rms_norm(x, g) * (1 + scale) + shift


def to_heads(t, n_heads):
    b, l, _ = t.shape
    return t.reshape(b, l, n_heads, -1).transpose(0, 2, 1, 3)


def hgrn_lower_bounds(lb_logits):
    p = jax.nn.softmax(lb_logits.astype(jnp.float32), axis=1)
    return jnp.cumsum(p, axis=1) - p[:, :1]


def gla_chunk_scan(q, k, v, logf, s0):
    b_, h_, l_, _ = q.shape
    dv = v.shape[-1]
    n = l_ // SCAN_CHUNK

    def chunks(a):
        return a.reshape(b_, h_, n, SCAN_CHUNK, a.shape[-1]).transpose(2, 0, 1, 3, 4)

    incl = jnp.tril(jnp.ones((SCAN_CHUNK, SCAN_CHUNK), dtype=bool))[:, :, None]

    def step(S, blk):
        qc, kc, vc, lc = blk
        b = jnp.cumsum(lc, axis=2)
        o_inter = jnp.einsum("bhtd,bhde->bhte", qc * jnp.exp(b), S)
        diff = b[:, :, :, None, :] - b[:, :, None, :, :]
        decay = jnp.exp(jnp.where(incl, diff, -jnp.inf))
        scores = jnp.sum(qc[:, :, :, None, :] * decay * kc[:, :, None, :, :], axis=-1)
        o = o_inter + jnp.einsum("bhts,bhse->bhte", scores, vc)
        b_last = b[:, :, -1:, :]
        S_new = jnp.exp(b_last[:, :, 0, :])[..., None] * S + jnp.einsum("bhsd,bhse->bhde", kc * jnp.exp(b_last - b), vc)
        return S_new, o

    s_fin, o = lax.scan(step, s0, (chunks(q), chunks(k), chunks(v), chunks(logf)))
    return o.transpose(1, 2, 0, 3, 4).reshape(b_, h_, l_, dv), s_fin


def hgrn_bidir(q_raw, i_raw, f_fwd, f_bwd, lb_f, lb_b, s0_f, s0_b):
    f32 = jnp.float32
    qh = to_heads(jax.nn.silu(q_raw).astype(f32), A_HEADS)
    vh = to_heads(i_raw.astype(f32), A_HEADS)

    def direction(f_raw, lb, s0, reverse):
        g = lb + (1 - lb) * jax.nn.sigmoid(f_raw.astype(f32))
        args = (qh, to_heads(1 - g, A_HEADS), vh, to_heads(jnp.log(g), A_HEADS))
        if reverse:
            args = tuple(jnp.flip(a, axis=2) for a in args)
        o, s = gla_chunk_scan(*args, s0.astype(f32))
        if reverse:
            o = jnp.flip(o, axis=2)
        return o, s

    o_f, s_f = direction(f_fwd, lb_f, s0_f, False)
    o_b, s_b = direction(f_bwd, lb_b, s0_b, True)
    return o_f + o_b, s_f, s_b


def hgrn_readout(o, g_out, norm_g):
    b_, h_, l_, dv = o.shape
    o = rms_norm(o.transpose(0, 2, 1, 3), norm_g).reshape(b_, l_, h_ * dv)
    return o.astype(g_out.dtype) * jax.nn.silu(g_out)


def axial_rope(x):
    l_ = x.shape[1]
    n_rows = l_ // GRID_W
    row = jnp.repeat(jnp.arange(n_rows), GRID_W)
    col = jnp.tile(jnp.arange(GRID_W), n_rows)
    half = HEAD_DIM // 2
    quarter = HEAD_DIM // 4
    inv = ROPE_THETA ** (-jnp.arange(quarter, dtype=jnp.float32) / quarter)

    def rot(xp, pos):
        ang = pos.astype(jnp.float32)[:, None] * inv[None, :]
        cos = jnp.cos(ang)[None, :, None, :]
        sin = jnp.sin(ang)[None, :, None, :]
        x1, x2 = xp[..., :quarter], xp[..., quarter:]
        return jnp.concatenate([x1 * cos - x2 * sin, x2 * cos + x1 * sin], axis=-1)

    xf = x.astype(jnp.float32)
    return jnp.concatenate([rot(xf[..., :half], row), rot(xf[..., half:], col)], axis=-1).astype(x.dtype)


def block_attention(q, k, v):
    b_, lq, h_, d = q.shape
    kvh = k.shape[2]
    grp = h_ // kvh
    nq = lq // Q_BLOCK
    scale = d ** -0.5
    qb = q.reshape(b_, nq, Q_BLOCK, kvh, grp, d).transpose(1, 0, 2, 3, 4, 5)

    def one(qblk):
        s = jnp.einsum("bqkgd,bskd->bkgqs", qblk, k).astype(jnp.float32) * scale
        p = jax.nn.softmax(s, axis=-1).astype(v.dtype)
        return jnp.einsum("bkgqs,bskd->bqkgd", p, v)

    o = lax.map(one, qb)
    return o.transpose(1, 0, 2, 3, 4, 5).reshape(b_, lq, h_, d)


def gmlp_chunk(u, v, norm_g, ws, bs):
    u = jax.nn.gelu(u)
    v = rms_norm(jax.nn.gelu(v), norm_g)
    b_, l_, _ = v.shape
    n = l_ // MIX_CHUNK
    vc = v.reshape(b_, n, MIX_CHUNK, C_GROUPS, C_GDIM)
    mixed = jnp.einsum("gts,bnsgc->bntgc", ws, vc) + bs.T[None, None, :, :, None]
    return u * mixed.reshape(b_, l_, C_WIDTH)


def split_points():
    pts, acc = [], 0
    for s in IN_SIZES[:-1]:
        acc += s
        pts.append(acc)
    return pts


def token_mixer(h, mp, ctx):
    (w_in_l, lb_f, lb_b, hgrn_norm_l, qn, kn, gn, ws, bs, w_pa_l, w_pb_l, w_pc_l, w_out_l) = mp
    b_, l_, _ = h.shape
    (a_q, a_ff, a_fb, a_i, a_g, b_q, b_k, b_v, c_u, c_v, gates) = jnp.split(h @ w_in_l, split_points(), axis=-1)
    q = rms_norm(b_q.reshape(b_, l_, B_HEADS, HEAD_DIM), qn)
    k = rms_norm(b_k.reshape(b_, l_, B_KV_HEADS, HEAD_DIM), kn)
    v = b_v.reshape(b_, l_, B_KV_HEADS, HEAD_DIM)
    if ctx is None:
        s0_f = jnp.zeros((b_, A_HEADS, A_DK, A_DV), jnp.float32)
        s0_b = s0_f
        y_b = block_attention(q, k, v)
    else:
        ck, cv, st = ctx
        s0_f, s0_b = st[:, 0], st[:, 1]
        k_all = jnp.concatenate([ck.astype(k.dtype), axial_rope(k)], axis=1)
        v_all = jnp.concatenate([cv.astype(v.dtype), v], axis=1)
        y_b = block_attention(axial_rope(q), k_all, v_all)
    y_b = y_b.reshape(b_, l_, B_WIDTH)
    o_a, s_f, s_b = hgrn_bidir(a_q, a_i, a_ff, a_fb, lb_f, lb_b, s0_f, s0_b)
    y_a = hgrn_readout(o_a, a_g, hgrn_norm_l)
    y_c = gmlp_chunk(c_u, c_v, gn, ws, bs)
    g_a, g_b, g_c = jnp.split(jax.nn.sigmoid(gates), N_BRANCH, axis=-1)
    merged = g_a * (y_a @ w_pa_l) + g_b * (y_b @ w_pb_l) + g_c * (y_c @ w_pc_l)
    out = merged @ w_out_l
    new_ctx = (k, v, jnp.stack([s_f, s_b], axis=1)) if ctx is None else None
    return out, new_ctx


def route(h, router_w, router_b):
    t = h.shape[0]
    probs = jax.nn.softmax((h @ router_w).astype(jnp.float32) + router_b.astype(jnp.float32), axis=-1)
    grp = probs.reshape(t, N_GROUPS, EXPERTS_PER_GROUP)
    grp_score = lax.top_k(grp, TOP_K)[0].sum(-1)
    gidx = lax.top_k(grp_score, 1)[1]
    in_group = jnp.arange(N_GROUPS)[None, :] == gidx
    masked = jnp.where(in_group[:, :, None], grp, -1.0).reshape(t, N_EXPERTS)
    w, idx = lax.top_k(masked, TOP_K)
    return idx, w / jnp.sum(w, axis=-1, keepdims=True)


def moe_ffn(h, router_w, router_b, w_gate, w_up, w_down):
    t, d = h.shape
    idx, wts = route(h, router_w, router_b)
    n_assign = t * TOP_K
    flat_e = idx.reshape(-1)
    flat_tok = jnp.arange(n_assign, dtype=jnp.int32) // TOP_K
    order = jnp.argsort(flat_e)
    e_sorted = flat_e[order]
    tok_sorted = flat_tok[order]
    w_sorted = wts.reshape(-1)[order]
    counts = jnp.zeros((N_EXPERTS,), jnp.int32).at[flat_e].add(1)
    start = jnp.cumsum(counts) - counts
    padded = (counts + MOE_BLOCK - 1) // MOE_BLOCK * MOE_BLOCK
    pad_end = jnp.cumsum(padded)
    pad_start = pad_end - padded
    dest = pad_start[e_sorted] + jnp.arange(n_assign, dtype=jnp.int32) - start[e_sorted]
    n_blocks = -(-(n_assign + N_EXPERTS * (MOE_BLOCK - 1)) // MOE_BLOCK)
    slot_tok = jnp.zeros((n_blocks * MOE_BLOCK,), jnp.int32).at[dest].set(tok_sorted)
    block_start = jnp.arange(n_blocks, dtype=jnp.int32) * MOE_BLOCK
    block_e = jnp.minimum(jnp.searchsorted(pad_end, block_start, side="right"), N_EXPERTS - 1)
    xb = h[slot_tok].reshape(n_blocks, MOE_BLOCK, d)

    def expert_block(args):
        xblk, e = args
        a = xblk @ w_gate[e]
        b = xblk @ w_up[e]
        return (jax.nn.silu(a) * b) @ w_down[e]

    yb = lax.map(expert_block, (xb, block_e)).reshape(n_blocks * MOE_BLOCK, d)
    contrib = yb[dest] * w_sorted[:, None].astype(h.dtype)
    return jnp.zeros_like(h).at[tok_sorted].add(contrib)


def trunk_layer(x, cond, lp, router_w, router_b, ctx):
    (w_mod_l, b_mod_l, n1, n2, mp, eg, eu, ed) = lp
    sh1, sc1, g1, sh2, sc2, g2 = ada_params(cond, w_mod_l, b_mod_l)
    out, new_ctx = token_mixer(modulate(x, n1, sh1, sc1), mp, ctx)
    x = x + g1 * out
    h2 = modulate(x, n2, sh2, sc2)
    b_, l_, d = h2.shape
    x = x + g2 * moe_ffn(h2.reshape(b_ * l_, d), router_w, router_b, eg, eu, ed).reshape(b_, l_, d)
    return x, new_ctx


def setup_inputs(seed: int = 0) -> dict:
    key = jax.random.key(seed)
    ks = jax.random.split(key, 32)
    f32 = jnp.float32
    D = D_MODEL

    def nrm(k, shape, scale):
        return jax.random.normal(k, shape, f32) * scale

    return {
        "x_prompt": nrm(ks[0], (BATCH, SEQ, D), 1.0),
        "x_sample": nrm(ks[1], (DEC_BATCH, DEC_SEQ, D), 1.0),
        "cache_k": nrm(ks[2], (DEC_BATCH, DEPTH, PAST_LEN, B_KV_HEADS, HEAD_DIM), 1.0),
        "cache_v": nrm(ks[3], (DEC_BATCH, DEPTH, PAST_LEN, B_KV_HEADS, HEAD_DIM), 1.0),
        "state_hgrn": nrm(ks[4], (DEC_BATCH, DEPTH, 2, A_HEADS, A_DK, A_DV), 0.3),
        "c": nrm(ks[5], (DEC_BATCH, D), 1.0),
        "c_ctx": nrm(ks[6], (D,), 1.0),
        "w_mod": nrm(ks[7], (DEPTH, D, N_MOD * D), 0.5 * D ** -0.5),
        "b_mod": nrm(ks[8], (DEPTH, N_MOD * D), 0.02),
        "norm1": 1.0 + nrm(ks[9], (DEPTH, D), 0.05),
        "norm2": 1.0 + nrm(ks[10], (DEPTH, D), 0.05),
        "norm_f": 1.0 + nrm(ks[11], (D,), 0.05),
        "w_in": nrm(ks[12], (DEPTH, D, N_IN), D ** -0.5),
        "hgrn_lb": nrm(ks[13], (2, DEPTH, A_WIDTH), 0.5),
        "hgrn_norm": 1.0 + nrm(ks[14], (DEPTH, A_DV), 0.05),
        "q_norm": 1.0 + nrm(ks[15], (DEPTH, HEAD_DIM), 0.05),
        "k_norm": 1.0 + nrm(ks[16], (DEPTH, HEAD_DIM), 0.05),
        "gmlp_norm": 1.0 + nrm(ks[17], (DEPTH, C_WIDTH), 0.05),
        "gmlp_ws": nrm(ks[18], (DEPTH, C_GROUPS, MIX_CHUNK, MIX_CHUNK), MIX_CHUNK ** -0.5),
        "gmlp_bs": 1.0 + nrm(ks[19], (DEPTH, C_GROUPS, MIX_CHUNK), 0.05),
        "w_pa": nrm(ks[20], (DEPTH, A_VWIDTH, D), A_VWIDTH ** -0.5),
        "w_pb": nrm(ks[21], (DEPTH, B_WIDTH, D), B_WIDTH ** -0.5),
        "w_pc": nrm(ks[22], (DEPTH, C_WIDTH, D), C_WIDTH ** -0.5),
        "w_out": nrm(ks[23], (DEPTH, D, D), D ** -0.5),
        "router_w": nrm(ks[24], (D, N_EXPERTS), D ** -0.5),
        "router_b": nrm(ks[25], (N_EXPERTS,), 0.01),
        "exp_gate": nrm(ks[26], (DEPTH, N_EXPERTS, D, EXPERT_FF), D ** -0.5),
        "exp_up": nrm(ks[27], (DEPTH, N_EXPERTS, D, EXPERT_FF), D ** -0.5),
        "exp_down": nrm(ks[28], (DEPTH, N_EXPERTS, EXPERT_FF, D), EXPERT_FF ** -0.5),
    }


def reference(x_prompt, x_sample, cache_k, cache_v, state_hgrn, c, c_ctx, w_mod, b_mod, norm1, norm2, norm_f,
              w_in, hgrn_lb, hgrn_norm, q_norm, k_norm, gmlp_norm, gmlp_ws, gmlp_bs, w_pa, w_pb, w_pc, w_out,
              router_w, router_b, exp_gate, exp_up, exp_down):
    lbs = hgrn_lower_bounds(hgrn_lb)

    def layer_params(l):
        mp = (w_in[l], lbs[0, l], lbs[1, l], hgrn_norm[l], q_norm[l], k_norm[l], gmlp_norm[l], gmlp_ws[l],
              gmlp_bs[l], w_pa[l], w_pb[l], w_pc[l], w_out[l])
        return (w_mod[l], b_mod[l], norm1[l], norm2[l], mp, exp_gate[l], exp_up[l], exp_down[l])

    xp = x_prompt
    ks_, vs_, ss_ = [], [], []
    for l in range(DEPTH):
        xp, (k_l, v_l, s_l) = trunk_layer(xp, c_ctx[None, :], layer_params(l), router_w, router_b, None)
        ks_.append(k_l)
        vs_.append(v_l)
        ss_.append(s_l)
    y_prompt = rms_norm(xp, norm_f)
    new_cache_k = jnp.stack(ks_, axis=1)
    new_cache_v = jnp.stack(vs_, axis=1)
    new_state_hgrn = jnp.stack(ss_, axis=1)

    xs = x_sample
    for l in range(DEPTH):
        xs, _ = trunk_layer(xs, c, layer_params(l), router_w, router_b,
                            (cache_k[:, l], cache_v[:, l], state_hgrn[:, l]))
    y_sample = rms_norm(xs, norm_f)
    return (y_prompt, y_sample, new_cache_k, new_cache_v, new_state_hgrn)
```

```python
import functools
import math

import jax
import jax.numpy as jnp
from jax import lax
from jax.experimental import pallas as pl
from jax.experimental.pallas import tpu as pltpu

F32 = jnp.float32
BF16 = jnp.bfloat16

EPS = 1e-6
N_MOD = 6
GRID_W = 64
ROPE_THETA = 10000.0
MIX_CHUNK = 128
N_GROUPS = 4
TOP_K = 2
LANE = 128
SUB_BLOCK = 16
COND_ROWS = 16
VMEM_LIMIT = 56 * 1024 * 1024

NT_DIMS = (((1,), (1,)), ((), ()))
TN_DIMS = (((0,), (0,)), ((), ()))


def _sigmoid(x):
    return 1.0 / (1.0 + jnp.exp(-x))


def _silu(x):
    return x * _sigmoid(x)


def _gelu_tanh(x):
    return 0.5 * x * (1.0 + jnp.tanh(math.sqrt(2.0 / math.pi) * (x + 0.044715 * (x * x * x))))


def _rms(x):
    return x * lax.rsqrt(jnp.mean(x * x, axis=-1, keepdims=True) + EPS)


def _split2(x):
    hi = x.astype(BF16)
    lo = (x - hi.astype(F32)).astype(BF16)
    return hi, lo


def _split3(x):
    hi = x.astype(BF16)
    r = x - hi.astype(F32)
    mid = r.astype(BF16)
    lo = (r - mid.astype(F32)).astype(BF16)
    return hi, mid, lo


def _params(sem):
    return pltpu.CompilerParams(dimension_semantics=sem, vmem_limit_bytes=VMEM_LIMIT)


def _mod_kernel(c_ref, w_ref, b_ref, o_ref):
    s = _silu(c_ref[...]).astype(BF16)
    o_ref[...] = jnp.dot(s, w_ref[...].astype(BF16), preferred_element_type=F32) + b_ref[...]


def _modulation(cond, w_mod, b_mod):
    depth, d, n = w_mod.shape
    tn = math.gcd(n, 1024)
    return pl.pallas_call(
        _mod_kernel,
        out_shape=jax.ShapeDtypeStruct((depth, COND_ROWS, n), F32),
        grid=(depth, n // tn),
        in_specs=[
            pl.BlockSpec((COND_ROWS, d), lambda l, j: (0, 0)),
            pl.BlockSpec((None, d, tn), lambda l, j: (l, 0, j)),
            pl.BlockSpec((None, 1, tn), lambda l, j: (l, 0, j)),
        ],
        out_specs=pl.BlockSpec((None, COND_ROWS, tn), lambda l, j: (l, 0, j)),
        compiler_params=_params(("parallel", "parallel")),
        name="adaln_modulation",
    )(cond, w_mod, b_mod.reshape(depth, 1, n))


class _Rows:
    def __init__(self, n_ctx, n_lat_seq, lat_len):
        self.n_ctx, self.lat_len = n_ctx, lat_len
        self.total = n_ctx + n_lat_seq * lat_len

    def cond_row(self, i, tm):
        ctx_tiles = self.n_ctx // tm
        per_seq = self.lat_len // tm
        return jnp.where(i < ctx_tiles, 0, 1 + (i - ctx_tiles) // per_seq)

    def mod_spec(self, k, tm, d):
        return pl.BlockSpec((None, 1, d), lambda i, *_: (self.cond_row(i, tm) * N_MOD + k, 0, 0))


def _in_proj_kernel(*refs, combine):
    if combine:
        x_ref, moe_ref, g2_ref, n_ref, sh_ref, sc_ref, w_ref, o_ref, x2_ref, h_scr = refs
    else:
        x_ref, n_ref, sh_ref, sc_ref, w_ref, o_ref, h_scr = refs

    @pl.when(pl.program_id(1) == 0)
    def _():
        x = x_ref[...]
        if combine:
            x = x + g2_ref[...] * moe_ref[...].astype(F32)
            x2_ref[...] = x
        h = _rms(x) * n_ref[...] * (1.0 + sc_ref[...]) + sh_ref[...]
        h_scr[...] = h.astype(BF16)

    o_ref[...] = jnp.dot(h_scr[...], w_ref[...], preferred_element_type=F32).astype(BF16)


def _in_proj(rows, x, moe, mod_prev, mod_l, norm_g, w_bf, tm, tn):
    t, d = x.shape
    n = w_bf.shape[1]
    combine = moe is not None
    row_spec = pl.BlockSpec((tm, d), lambda i, j: (i, 0))
    vec_spec = pl.BlockSpec((1, d), lambda i, j: (0, 0))
    in_specs = [row_spec]
    args = [x]
    if combine:
        in_specs += [row_spec, rows.mod_spec(5, tm, d)]
        args += [moe, mod_prev]
    in_specs += [vec_spec, rows.mod_spec(0, tm, d), rows.mod_spec(1, tm, d),
                 pl.BlockSpec((d, tn), lambda i, j: (0, j))]
    args += [norm_g.reshape(1, d), mod_l, mod_l, w_bf]
    out_shape = [jax.ShapeDtypeStruct((t, n), BF16)]
    out_specs = [pl.BlockSpec((tm, tn), lambda i, j: (i, j))]
    if combine:
        out_shape.append(jax.ShapeDtypeStruct((t, d), F32))
        out_specs.append(row_spec)
    res = pl.pallas_call(
        functools.partial(_in_proj_kernel, combine=combine),
        out_shape=out_shape,
        grid=(t // tm, n // tn),
        in_specs=in_specs,
        out_specs=out_specs,
        scratch_shapes=[pltpu.VMEM((tm, d), BF16)],
        compiler_params=_params(("parallel", "arbitrary")),
        name="in_proj",
    )(*args)
    return (res[0], res[1]) if combine else (res[0], x)


def _rope(x, cos, sin_signed):
    lane = lax.broadcasted_iota(jnp.int32, x.shape, 1)
    first = (lane // (LANE // 4)) % 2 == 0
    swapped = jnp.where(first, pltpu.roll(x, LANE - LANE // 4, 1), pltpu.roll(x, LANE // 4, 1))
    return x * cos + swapped * sin_signed


def _attn_kernel(*refs, latent, past, tq, grp, scale):
    if latent:
        (q_ref, kn_ref, vn_ref, ck_ref, cv_ref, cos_ref, sin_ref, qg_ref, kg_ref,
         o_ref, k_scr, v_scr) = refs
    else:
        q_ref, kn_ref, vn_ref, qg_ref, kg_ref, o_ref, ko_ref, vo_ref, k_scr, v_scr = refs
    qi = pl.program_id(2)

    @pl.when(qi == 0)
    def _():
        k = _rms(kn_ref[...].astype(F32)) * kg_ref[...]
        if latent:
            k = _rope(k, cos_ref[...], sin_ref[...])
            k_scr[0:past, :] = ck_ref[...].astype(BF16)
            v_scr[0:past, :] = cv_ref[...].astype(BF16)
        else:
            ko_ref[...] = k
            vo_ref[...] = vn_ref[...].astype(F32)
        k_scr[past:, :] = k.astype(BF16)
        v_scr[past:, :] = vn_ref[...]

    q = q_ref[...].astype(F32)
    heads = []
    for h in range(grp):
        qh = _rms(q[:, h * LANE:(h + 1) * LANE]) * qg_ref[...]
        if latent:
            r0 = pl.multiple_of(qi * tq, tq)
            qh = _rope(qh, cos_ref[pl.ds(r0, tq), :], sin_ref[pl.ds(r0, tq), :])
        heads.append((qh * scale).astype(BF16))
    qs = jnp.concatenate(heads, axis=0)
    s = lax.dot_general(qs, k_scr[...], NT_DIMS, preferred_element_type=F32)
    p = jnp.exp(s - jnp.max(s, axis=-1, keepdims=True))
    denom = jnp.sum(p, axis=-1, keepdims=True)
    o = jnp.dot(p.astype(BF16), v_scr[...], preferred_element_type=F32) / denom
    for h in range(grp):
        o_ref[:, h * LANE:(h + 1) * LANE] = o[h * tq:(h + 1) * tq, :].astype(BF16)


def _attention(p_all, col, row0, n_seq, seq_len, n_heads, kvh, q_gain, k_gain, tq, ctx=None):
    grp = n_heads // kvh
    latent = ctx is not None
    qw = grp * LANE
    assert col["b_q"] % qw == 0 and row0 % seq_len == 0 and seq_len % tq == 0
    seq0, nq = row0 // seq_len, seq_len // tq
    q_spec = pl.BlockSpec((tq, qw), lambda b, g, i: (seq0 * nq + b * nq + i, col["b_q"] // qw + g))
    k_spec = pl.BlockSpec((seq_len, LANE), lambda b, g, i: (seq0 + b, col["b_k"] // LANE + g))
    v_spec = pl.BlockSpec((seq_len, LANE), lambda b, g, i: (seq0 + b, col["b_v"] // LANE + g))
    gain_spec = pl.BlockSpec((1, LANE), lambda b, g, i: (0, 0))
    in_specs = [q_spec, k_spec, v_spec]
    args = [p_all, p_all, p_all]
    past = 0
    if latent:
        cache_k, cache_v, layer, cos, sin = ctx
        past = cache_k.shape[2]
        c_spec = pl.BlockSpec((None, None, past, LANE), lambda b, g, i: (b, layer, 0, g))
        t_spec = pl.BlockSpec((seq_len, LANE), lambda b, g, i: (0, 0))
        in_specs += [c_spec, c_spec, t_spec, t_spec]
        args += [cache_k, cache_v, cos, sin]
    in_specs += [gain_spec, gain_spec]
    args += [q_gain.reshape(1, LANE), k_gain.reshape(1, LANE)]
    out_shape = [jax.ShapeDtypeStruct((n_seq * seq_len, n_heads * LANE), BF16)]
    out_specs = [pl.BlockSpec((tq, qw), lambda b, g, i: (b * nq + i, g))]
    if not latent:
        kv_shape = jax.ShapeDtypeStruct((n_seq, seq_len, kvh * LANE), F32)
        kv_spec = pl.BlockSpec((None, seq_len, LANE), lambda b, g, i: (b, 0, g))
        out_shape += [kv_shape, kv_shape]
        out_specs += [kv_spec, kv_spec]
    return pl.pallas_call(
        functools.partial(_attn_kernel, latent=latent, past=past, tq=tq, grp=grp, scale=LANE ** -0.5),
        out_shape=out_shape,
        grid=(n_seq, kvh, nq),
        in_specs=in_specs,
        out_specs=out_specs,
        scratch_shapes=[pltpu.VMEM((past + seq_len, LANE), BF16), pltpu.VMEM((past + seq_len, LANE), BF16)],
        compiler_params=_params(("parallel", "parallel", "arbitrary")),
        name="attention_latent" if latent else "attention_context",
    )(*args)


def _rope_tables(seq_len):
    quarter = LANE // 4
    inv = ROPE_THETA ** (-jnp.arange(quarter, dtype=F32) / quarter)
    pos = jnp.arange(seq_len)
    row = (pos // GRID_W).astype(F32)[:, None] * inv[None, :]
    colp = (pos % GRID_W).astype(F32)[:, None] * inv[None, :]
    cos = jnp.concatenate([jnp.cos(row), jnp.cos(row), jnp.cos(colp), jnp.cos(colp)], axis=-1)
    sin = jnp.concatenate([-jnp.sin(row), jnp.sin(row), -jnp.sin(colp), jnp.sin(colp)], axis=-1)
    return cos, sin


def _hgrn_kernel(q_ref, ff_ref, fb_ref, i_ref, g_ref, lb_ref, ng_ref, s0_ref, y_ref, sfin_ref,
                 of_scr, ob_scr, sf_scr, sb_scr, *, seq_len):
    sb = SUB_BLOCK
    n_sub = seq_len // sb
    r_id = lax.broadcasted_iota(jnp.int32, (sb, sb), 0)
    c_id = lax.broadcasted_iota(jnp.int32, (sb, sb), 1)
    prefix = (r_id >= c_id).astype(BF16)
    suffix = (r_id <= c_id).astype(BF16)
    ones = jnp.ones((LANE, LANE), BF16)
    row_id = lax.broadcasted_iota(jnp.int32, (sb, LANE), 0)

    sf_scr[...] = s0_ref[0].T
    sb_scr[...] = s0_ref[1].T

    def block(r0, f_ref, lb, st_scr, tri, forward):
        q = _silu(q_ref[pl.ds(r0, sb), :].astype(F32))
        g = lb + (1.0 - lb) * _sigmoid(f_ref[pl.ds(r0, sb), :].astype(F32))
        k = 1.0 - g
        v = i_ref[pl.ds(r0, sb), :]
        vf = v.astype(F32)
        hi, mid, lo = _split3(jnp.log(g))
        c = (jnp.dot(tri, hi, preferred_element_type=F32) + jnp.dot(tri, mid, preferred_element_type=F32)
             + jnp.dot(tri, lo, preferred_element_type=F32))
        st = st_scr[...]
        o = lax.dot_general((q * jnp.exp(c)).astype(BF16), st.astype(BF16), NT_DIMS,
                            preferred_element_type=F32)
        terms = []
        for s in range(sb):
            live = (row_id >= s) if forward else (row_id <= s)
            decay = jnp.where(live, jnp.exp(c - c[s:s + 1, :]), 0.0)
            terms.append((q * k[s:s + 1, :] * decay).astype(BF16))
        score = jnp.dot(jnp.concatenate(terms, axis=0), ones, preferred_element_type=F32)
        for s in range(sb):
            o = o + score[s * sb:(s + 1) * sb, :] * vf[s:s + 1, :]
        c_end = c[sb - 1:sb, :] if forward else c[0:1, :]
        kd = (k * jnp.exp(c_end - c)).astype(BF16)
        st_scr[...] = st * jnp.exp(c_end) + lax.dot_general(v, kd, TN_DIMS, preferred_element_type=F32)
        return o

    lb_f, lb_b = lb_ref[0:1, :], lb_ref[1:2, :]

    def body(j, carry):
        r0 = pl.multiple_of(j * sb, sb)
        of_scr[pl.ds(r0, sb), :] = block(r0, ff_ref, lb_f, sf_scr, prefix, True)
        r1 = pl.multiple_of((n_sub - 1 - j) * sb, sb)
        ob_scr[pl.ds(r1, sb), :] = block(r1, fb_ref, lb_b, sb_scr, suffix, False)
        return carry

    lax.fori_loop(0, n_sub, body, 0)
    sfin_ref[0] = sf_scr[...].T
    sfin_ref[1] = sb_scr[...].T
    o = _rms(of_scr[...] + ob_scr[...]) * ng_ref[...]
    y_ref[...] = (o * _silu(g_ref[...].astype(F32))).astype(BF16)


def _hgrn(p_all, col, row0, n_seq, seq_len, n_heads, lb_l, norm_g, s0):
    assert row0 % seq_len == 0 and seq_len % SUB_BLOCK == 0
    seq0 = row0 // seq_len

    def cspec(name):
        return pl.BlockSpec((seq_len, LANE), lambda b, h: (seq0 + b, col[name] // LANE + h))

    st_spec = pl.BlockSpec((None, 2, None, LANE, LANE), lambda b, h: (b, 0, h, 0, 0))
    return pl.pallas_call(
        functools.partial(_hgrn_kernel, seq_len=seq_len),
        out_shape=[jax.ShapeDtypeStruct((n_seq * seq_len, n_heads * LANE), BF16),
                   jax.ShapeDtypeStruct(s0.shape, F32)],
        grid=(n_seq, n_heads),
        in_specs=[cspec("a_q"), cspec("a_ff"), cspec("a_fb"), cspec("a_i"), cspec("a_g"),
                  pl.BlockSpec((2, LANE), lambda b, h: (0, h)),
                  pl.BlockSpec((1, LANE), lambda b, h: (0, 0)),
                  st_spec],
        out_specs=[pl.BlockSpec((seq_len, LANE), lambda b, h: (b, h)), st_spec],
        scratch_shapes=[pltpu.VMEM((seq_len, LANE), F32), pltpu.VMEM((seq_len, LANE), F32),
                        pltpu.VMEM((LANE, LANE), F32), pltpu.VMEM((LANE, LANE), F32)],
        compiler_params=_params(("parallel", "parallel")),
        name="hgrn_%d" % seq_len,
    )(p_all, p_all, p_all, p_all, p_all, lb_l, norm_g.reshape(1, LANE), s0)


def _gmlp_kernel(u_ref, v_ref, gn_ref, ws_ref, bs_ref, o_ref, *, n_chunks, groups):
    u = _gelu_tanh(u_ref[...].astype(F32))
    v = (_rms(_gelu_tanh(v_ref[...].astype(F32))) * gn_ref[...]).astype(BF16)
    for n in range(n_chunks):
        rs = slice(n * MIX_CHUNK, (n + 1) * MIX_CHUNK)
        for g in range(groups):
            cs = slice(g * LANE, (g + 1) * LANE)
            mixed = jnp.dot(ws_ref[g], v[rs, cs], preferred_element_type=F32) + bs_ref[g]
            o_ref[rs, cs] = (u[rs, cs] * mixed).astype(BF16)


def _gmlp(p_all, col, norm_g, ws_bf, bias, tm):
    t = p_all.shape[0]
    groups = ws_bf.shape[0]
    cw = groups * LANE
    assert col["c_u"] % cw == 0 and col["c_v"] % cw == 0
    return pl.pallas_call(
        functools.partial(_gmlp_kernel, n_chunks=tm // MIX_CHUNK, groups=groups),
        out_shape=jax.ShapeDtypeStruct((t, cw), BF16),
        grid=(t // tm,),
        in_specs=[pl.BlockSpec((tm, cw), lambda i: (i, col["c_u"] // cw)),
                  pl.BlockSpec((tm, cw), lambda i: (i, col["c_v"] // cw)),
                  pl.BlockSpec((1, cw), lambda i: (0, 0)),
                  pl.BlockSpec((groups, MIX_CHUNK, MIX_CHUNK), lambda i: (0, 0, 0)),
                  pl.BlockSpec((groups, MIX_CHUNK, LANE), lambda i: (0, 0, 0))],
        out_specs=pl.BlockSpec((tm, cw), lambda i: (i, 0)),
        compiler_params=_params(("parallel",)),
        name="gmlp",
    )(p_all, p_all, norm_g.reshape(1, cw), ws_bf, bias)


def _merge_kernel(x_ref, ya_ref, yb_ref, yc_ref, ga0, ga1, gb0, gb1, gc0, gc1, wpa_ref, wpb_ref, wpc_ref,
                  wout_ref, g1_ref, n2_ref, sh2_ref, sc2_ref, rhi_ref, rlo_ref, x1_ref, h2_ref, lg_ref):
    def gate(r0, r1):
        return jnp.concatenate([_sigmoid(r0[...].astype(F32)), _sigmoid(r1[...].astype(F32))], axis=-1)

    merged = gate(ga0, ga1) * jnp.dot(ya_ref[...], wpa_ref[...], preferred_element_type=F32)
    merged += gate(gb0, gb1) * jnp.dot(yb_ref[...], wpb_ref[...], preferred_element_type=F32)
    merged += gate(gc0, gc1) * jnp.dot(yc_ref[...], wpc_ref[...], preferred_element_type=F32)
    out = jnp.dot(merged.astype(BF16), wout_ref[...], preferred_element_type=F32)
    x1 = x_ref[...] + g1_ref[...] * out
    x1_ref[...] = x1
    h2 = _rms(x1) * n2_ref[...] * (1.0 + sc2_ref[...]) + sh2_ref[...]
    h2_ref[...] = h2.astype(BF16)
    hi, lo = _split2(h2)
    rhi, rlo = rhi_ref[...], rlo_ref[...]
    lg_ref[...] = (jnp.dot(hi, rhi, preferred_element_type=F32) + jnp.dot(hi, rlo, preferred_element_type=F32)
                   + jnp.dot(lo, rhi, preferred_element_type=F32) + jnp.dot(lo, rlo, preferred_element_type=F32))


def _merge(rows, x, y_a, y_b, y_c, p_all, col, mod_l, norm2_g, w_pa, w_pb, w_pc, w_out, r_hi, r_lo, tm):
    t, d = x.shape
    gw = d // 2
    assert col["gates"] % gw == 0
    g0 = col["gates"] // gw
    row = lambda w: pl.BlockSpec((tm, w), lambda i: (i, 0))
    full = lambda a: pl.BlockSpec(a.shape, lambda i: (0,) * a.ndim)
    gate_specs = [pl.BlockSpec((tm, gw), functools.partial(lambda i, k: (i, g0 + k), k=k)) for k in range(6)]
    return pl.pallas_call(
        _merge_kernel,
        out_shape=[jax.ShapeDtypeStruct((t, d), F32), jax.ShapeDtypeStruct((t, d), BF16),
                   jax.ShapeDtypeStruct((t, LANE), F32)],
        grid=(t // tm,),
        in_specs=[row(d), row(y_a.shape[1]), row(y_b.shape[1]), row(y_c.shape[1])] + gate_specs
                 + [full(w_pa), full(w_pb), full(w_pc), full(w_out), rows.mod_spec(2, tm, d),
                    pl.BlockSpec((1, d), lambda i: (0, 0)), rows.mod_spec(3, tm, d), rows.mod_spec(4, tm, d),
                    full(r_hi), full(r_lo)],
        out_specs=[row(d), row(d), row(LANE)],
        compiler_params=_params(("parallel",)),
        name="merge_out_proj",
    )(x, y_a, y_b, y_c, *([p_all] * 6), w_pa, w_pb, w_pc, w_out, mod_l, norm2_g.reshape(1, d), mod_l, mod_l,
      r_hi, r_lo)


def _moe_kernel(be_ref, bv_ref, x_ref, wg_ref, wu_ref, wd_ref, o_ref):
    i = pl.program_id(0)

    @pl.when(bv_ref[i] > 0)
    def _():
        x = x_ref[...]
        a = jnp.dot(x, wg_ref[...], preferred_element_type=F32)
        b = jnp.dot(x, wu_ref[...], preferred_element_type=F32)
        h = (_silu(a) * b).astype(BF16)
        o_ref[...] = jnp.dot(h, wd_ref[...], preferred_element_type=F32).astype(BF16)

    @pl.when(bv_ref[i] == 0)
    def _():
        o_ref[...] = jnp.zeros_like(o_ref)


def _moe_ffn(xb, block_e, block_valid, wg, wu, wd, tm):
    n_slots, d = xb.shape
    ff = wg.shape[2]
    grid_spec = pltpu.PrefetchScalarGridSpec(
        num_scalar_prefetch=2,
        grid=(n_slots // tm,),
        in_specs=[pl.BlockSpec((tm, d), lambda i, be, bv: (i, 0)),
                  pl.BlockSpec((None, d, ff), lambda i, be, bv: (be[i], 0, 0)),
                  pl.BlockSpec((None, d, ff), lambda i, be, bv: (be[i], 0, 0)),
                  pl.BlockSpec((None, ff, d), lambda i, be, bv: (be[i], 0, 0))],
        out_specs=pl.BlockSpec((tm, d), lambda i, be, bv: (i, 0)),
    )
    return pl.pallas_call(
        _moe_kernel,
        out_shape=jax.ShapeDtypeStruct((n_slots, d), BF16),
        grid_spec=grid_spec,
        compiler_params=_params(("arbitrary",)),
        name="moe_experts",
    )(block_e, block_valid, xb, wg, wu, wd)


def _route(logits, router_b):
    t, e = logits.shape
    per = e // N_GROUPS
    probs = jax.nn.softmax(logits + router_b.astype(F32), axis=-1)
    grp = probs.reshape(t, N_GROUPS, per)
    grp_score = lax.top_k(grp, TOP_K)[0].sum(-1)
    gidx = lax.top_k(grp_score, 1)[1]
    in_group = jnp.arange(N_GROUPS)[None, :] == gidx
    masked = jnp.where(in_group[:, :, None], grp, -1.0).reshape(t, e)
    w, idx = lax.top_k(masked, TOP_K)
    return idx, w / jnp.sum(w, axis=-1, keepdims=True)


def _dispatch(idx, n_experts, tm):
    t = idx.shape[0]
    flat_e = idx.reshape(-1)
    n_assign = flat_e.shape[0]
    onehot = (flat_e[:, None] == jnp.arange(n_experts)[None, :]).astype(jnp.int32)
    rank = jnp.sum((jnp.cumsum(onehot, axis=0) - onehot) * onehot, axis=-1)
    counts = jnp.sum(onehot, axis=0)
    padded = (counts + tm - 1) // tm * tm
    pad_end = jnp.cumsum(padded)
    pad_start = pad_end - padded
    dest = pad_start[flat_e] + rank
    n_blocks = -(-(n_assign + n_experts * (tm - 1)) // tm)
    tok = jnp.arange(n_assign, dtype=jnp.int32) // TOP_K
    slot_tok = jnp.zeros((n_blocks * tm,), jnp.int32).at[dest].set(tok)
    block_start = jnp.arange(n_blocks, dtype=jnp.int32) * tm
    block_e = jnp.minimum(jnp.searchsorted(pad_end, block_start, side="right"), n_experts - 1).astype(jnp.int32)
    block_valid = (block_start < pad_end[-1]).astype(jnp.int32)
    return slot_tok, dest.reshape(t, TOP_K), block_e, block_valid


def _final_kernel(x_ref, moe_ref, g2_ref, nf_ref, o_ref):
    x = x_ref[...] + g2_ref[...] * moe_ref[...].astype(F32)
    o_ref[...] = _rms(x) * nf_ref[...]


def _final_norm(rows, x, moe, mod_l, norm_f, row0, n_rows, tm):
    d = x.shape[1]
    t0 = row0 // tm
    row_spec = pl.BlockSpec((tm, d), lambda i: (t0 + i, 0))
    g2_spec = pl.BlockSpec((None, 1, d), lambda i: (rows.cond_row(t0 + i, tm) * N_MOD + 5, 0, 0))
    return pl.pallas_call(
        _final_kernel,
        out_shape=jax.ShapeDtypeStruct((n_rows, d), F32),
        grid=(n_rows // tm,),
        in_specs=[row_spec, row_spec, g2_spec, pl.BlockSpec((1, d), lambda i: (0, 0))],
        out_specs=pl.BlockSpec((tm, d), lambda i: (i, 0)),
        compiler_params=_params(("parallel",)),
        name="final_norm",
    )(x, moe, mod_l, norm_f.reshape(1, d))


def kernel(x_prompt, x_sample, cache_k, cache_v, state_hgrn, c, c_ctx, w_mod, b_mod, norm1, norm2, norm_f,
           w_in, hgrn_lb, hgrn_norm, q_norm, k_norm, gmlp_norm, gmlp_ws, gmlp_bs, w_pa, w_pb, w_pc, w_out,
           router_w, router_b, exp_gate, exp_up, exp_down):
    n_ctx_seq, ctx_len, d = x_prompt.shape
    n_lat_seq, lat_len, _ = x_sample.shape
    depth = w_mod.shape[0]
    past, kvh = cache_k.shape[2], cache_k.shape[3]
    a_heads = state_hgrn.shape[3]
    a_width, b_width, c_width = hgrn_lb.shape[2], w_pb.shape[1], w_pc.shape[1]
    b_heads = b_width // LANE
    n_experts = router_w.shape[1]
    assert cache_k.shape[4] == LANE and state_hgrn.shape[4:] == (LANE, LANE) and a_width == a_heads * LANE

    sizes = [("a_q", a_width), ("a_ff", a_width), ("a_fb", a_width), ("a_i", a_width), ("a_g", a_width),
             ("b_q", b_width), ("b_k", kvh * LANE), ("b_v", kvh * LANE), ("c_u", c_width), ("c_v", c_width),
             ("gates", 3 * d)]
    col, acc = {}, 0
    for name, width in sizes:
        col[name] = acc
        acc += width
    assert acc == w_in.shape[2]

    n_ctx = n_ctx_seq * ctx_len
    rows = _Rows(n_ctx, n_lat_seq, lat_len)
    tm_proj = min(512, n_ctx, lat_len)
    tn_proj = min(1024, math.gcd(w_in.shape[2], 1024))
    tm_row = min(256, n_ctx, lat_len)
    tm_moe = 256
    tq_lat = min(128, lat_len)

    cond = jnp.concatenate([c_ctx[None, :], c, jnp.zeros((COND_ROWS - 1 - n_lat_seq, d), F32)], axis=0)
    mod = _modulation(cond, w_mod, b_mod).reshape(depth, COND_ROWS * N_MOD, 1, d)

    p_lb = jax.nn.softmax(hgrn_lb.astype(F32), axis=1)
    lbs = jnp.cumsum(p_lb, axis=1) - p_lb[:, :1]

    cos, sin = _rope_tables(lat_len)
    cache_k2 = cache_k.reshape(n_lat_seq, depth, past, kvh * LANE)
    cache_v2 = cache_v.reshape(n_lat_seq, depth, past, kvh * LANE)
    s0_ctx = jnp.zeros((n_ctx_seq,) + state_hgrn.shape[2:], F32)
    r_pad = jnp.zeros((d, LANE), F32).at[:, :n_experts].set(router_w)
    r_hi, r_lo = _split2(r_pad)
    gmlp_bias = jnp.broadcast_to(gmlp_bs[:, :, :, None], gmlp_bs.shape + (LANE,))

    x = jnp.concatenate([x_prompt.reshape(n_ctx, d), x_sample.reshape(n_lat_seq * lat_len, d)], axis=0)
    moe = None
    new_k, new_v, new_s = [], [], []
    for l in range(depth):
        mod_l = mod[l]
        p_all, x = _in_proj(rows, x, moe, mod[l - 1] if l else None, mod_l, norm1[l], w_in[l].astype(BF16),
                            tm_proj, tn_proj)
        yb_ctx, k_l, v_l = _attention(p_all, col, 0, n_ctx_seq, ctx_len, b_heads, kvh, q_norm[l], k_norm[l],
                                      ctx_len)
        (yb_lat,) = _attention(p_all, col, n_ctx, n_lat_seq, lat_len, b_heads, kvh, q_norm[l], k_norm[l],
                               tq_lat, ctx=(cache_k2, cache_v2, l, cos, sin))
        ya_ctx, s_l = _hgrn(p_all, col, 0, n_ctx_seq, ctx_len, a_heads, lbs[:, l], hgrn_norm[l], s0_ctx)
        ya_lat, _ = _hgrn(p_all, col, n_ctx, n_lat_seq, lat_len, a_heads, lbs[:, l], hgrn_norm[l],
                          state_hgrn[:, l])
        y_c = _gmlp(p_all, col, gmlp_norm[l], gmlp_ws[l].astype(BF16), gmlp_bias[l], tm_proj)
        x, h2, logits = _merge(rows, x, jnp.concatenate([ya_ctx, ya_lat], axis=0),
                               jnp.concatenate([yb_ctx, yb_lat], axis=0), y_c, p_all, col, mod_l, norm2[l],
                               w_pa[l].astype(BF16), w_pb[l].astype(BF16), w_pc[l].astype(BF16),
                               w_out[l].astype(BF16), r_hi, r_lo, tm_row)
        idx, wts = _route(logits[:, :n_experts], router_b)
        slot_tok, dest, block_e, block_valid = _dispatch(idx, n_experts, tm_moe)
        yb = _moe_ffn(jnp.take(h2, slot_tok, axis=0), block_e, block_valid, exp_gate[l].astype(BF16),
                      exp_up[l].astype(BF16), exp_down[l].astype(BF16), tm_moe)
        moe = (jnp.take(yb, dest[:, 0], axis=0).astype(F32) * wts[:, 0:1]
               + jnp.take(yb, dest[:, 1], axis=0).astype(F32) * wts[:, 1:2]).astype(BF16)
        new_k.append(k_l)
        new_v.append(v_l)
        new_s.append(s_l)

    y_ctx = _final_norm(rows, x, moe, mod[depth - 1], norm_f, 0, n_ctx, tm_row)
    y_lat = _final_norm(rows, x, moe, mod[depth - 1], norm_f, n_ctx, n_lat_seq * lat_len, tm_row)
    kv_shape = (n_ctx_seq, depth, ctx_len, kvh, LANE)
    return (y_ctx.reshape(n_ctx_seq, ctx_len, d), y_lat.reshape(n_lat_seq, lat_len, d),
            jnp.stack(new_k, axis=1).reshape(kv_shape), jnp.stack(new_v, axis=1).reshape(kv_shape),
            jnp.stack(new_s, axis=1))
```

```python
import functools
import math

import jax
import jax.numpy as jnp
from jax import lax
from jax.experimental import pallas as pl
from jax.experimental.pallas import tpu as pltpu

F32 = jnp.float32
BF16 = jnp.bfloat16

EPS = 1e-6
N_MOD = 6
GRID_W = 64
ROPE_THETA = 10000.0
MIX_CHUNK = 128
N_GROUPS = 4
TOP_K = 2
LANE = 128
SUB_BLOCK = 16
HGRN_BLOCK = 128
LOG2E = 1.4426950408889634
COND_ROWS = 16
VMEM_LIMIT = 56 * 1024 * 1024

NT_DIMS = (((1,), (1,)), ((), ()))
TN_DIMS = (((0,), (0,)), ((), ()))


def _sigmoid(x):
    return 1.0 / (1.0 + jnp.exp(-x))


def _silu(x):
    return x * _sigmoid(x)


def _gelu_tanh(x):
    return 0.5 * x * (1.0 + jnp.tanh(math.sqrt(2.0 / math.pi) * (x + 0.044715 * (x * x * x))))


def _rms(x):
    return x * lax.rsqrt(jnp.mean(x * x, axis=-1, keepdims=True) + EPS)


def _split2(x):
    hi = x.astype(BF16)
    lo = (x - hi.astype(F32)).astype(BF16)
    return hi, lo


def _split3(x):
    hi = x.astype(BF16)
    r = x - hi.astype(F32)
    mid = r.astype(BF16)
    lo = (r - mid.astype(F32)).astype(BF16)
    return hi, mid, lo


def _params(sem):
    return pltpu.CompilerParams(dimension_semantics=sem, vmem_limit_bytes=VMEM_LIMIT)


def _mod_kernel(c_ref, w_ref, b_ref, o_ref):
    s = _silu(c_ref[...]).astype(BF16)
    o_ref[...] = jnp.dot(s, w_ref[...].astype(BF16), preferred_element_type=F32) + b_ref[...]


def _modulation(cond, w_mod, b_mod):
    depth, d, n = w_mod.shape
    tn = math.gcd(n, 1024)
    return pl.pallas_call(
        _mod_kernel,
        out_shape=jax.ShapeDtypeStruct((depth, COND_ROWS, n), F32),
        grid=(depth, n // tn),
        in_specs=[
            pl.BlockSpec((COND_ROWS, d), lambda l, j: (0, 0)),
            pl.BlockSpec((None, d, tn), lambda l, j: (l, 0, j)),
            pl.BlockSpec((None, 1, tn), lambda l, j: (l, 0, j)),
        ],
        out_specs=pl.BlockSpec((None, COND_ROWS, tn), lambda l, j: (l, 0, j)),
        compiler_params=_params(("parallel", "parallel")),
        name="adaln_modulation",
    )(cond, w_mod, b_mod.reshape(depth, 1, n))


class _Rows:
    def __init__(self, n_ctx, n_lat_seq, lat_len):
        self.n_ctx, self.lat_len = n_ctx, lat_len
        self.total = n_ctx + n_lat_seq * lat_len

    def cond_row(self, i, tm):
        ctx_tiles = self.n_ctx // tm
        per_seq = self.lat_len // tm
        return jnp.where(i < ctx_tiles, 0, 1 + (i - ctx_tiles) // per_seq)

    def mod_spec(self, k, tm, d):
        return pl.BlockSpec((None, 1, d), lambda i, *_: (self.cond_row(i, tm) * N_MOD + k, 0, 0))


def _in_proj_kernel(*refs, combine):
    if combine:
        x_ref, moe_ref, g2_ref, n_ref, sh_ref, sc_ref, w_ref, o_ref, x2_ref, h_scr = refs
    else:
        x_ref, n_ref, sh_ref, sc_ref, w_ref, o_ref, h_scr = refs

    @pl.when(pl.program_id(1) == 0)
    def _():
        x = x_ref[...]
        if combine:
            x = x + g2_ref[...] * moe_ref[...].astype(F32)
            x2_ref[...] = x
        h = _rms(x) * n_ref[...] * (1.0 + sc_ref[...]) + sh_ref[...]
        h_scr[...] = h.astype(BF16)

    o_ref[...] = jnp.dot(h_scr[...], w_ref[...], preferred_element_type=F32).astype(BF16)


def _in_proj(rows, x, moe, mod_prev, mod_l, norm_g, w_bf, tm, tn):
    t, d = x.shape
    n = w_bf.shape[1]
    combine = moe is not None
    row_spec = pl.BlockSpec((tm, d), lambda i, j: (i, 0))
    row_in_spec = pl.BlockSpec((tm, d), lambda i, j: (i, 0), pipeline_mode=pl.Buffered(1))
    vec_spec = pl.BlockSpec((1, d), lambda i, j: (0, 0))
    in_specs = [row_in_spec]
    args = [x]
    if combine:
        in_specs += [row_in_spec, rows.mod_spec(5, tm, d)]
        args += [moe, mod_prev]
    in_specs += [vec_spec, rows.mod_spec(0, tm, d), rows.mod_spec(1, tm, d),
                 pl.BlockSpec((d, tn), lambda i, j: (0, j))]
    args += [norm_g.reshape(1, d), mod_l, mod_l, w_bf]
    out_shape = [jax.ShapeDtypeStruct((t, n), BF16)]
    out_specs = [pl.BlockSpec((tm, tn), lambda i, j: (i, j))]
    if combine:
        out_shape.append(jax.ShapeDtypeStruct((t, d), F32))
        out_specs.append(row_spec)
    res = pl.pallas_call(
        functools.partial(_in_proj_kernel, combine=combine),
        out_shape=out_shape,
        grid=(t // tm, n // tn),
        in_specs=in_specs,
        out_specs=out_specs,
        scratch_shapes=[pltpu.VMEM((tm, d), BF16)],
        compiler_params=_params(("parallel", "arbitrary")),
        name="in_proj",
    )(*args)
    return (res[0], res[1]) if combine else (res[0], x)


def _rope(x, cos, sin_signed):
    lane = lax.broadcasted_iota(jnp.int32, x.shape, 1)
    first = (lane // (LANE // 4)) % 2 == 0
    swapped = jnp.where(first, pltpu.roll(x, LANE - LANE // 4, 1), pltpu.roll(x, LANE // 4, 1))
    return x * cos + swapped * sin_signed


def _attn_kernel(*refs, latent, past, tq, grp, scale):
    if latent:
        (q_ref, kn_ref, vn_ref, ck_ref, cv_ref, cos_ref, sin_ref, qg_ref, kg_ref,
         o_ref, k_scr, v_scr) = refs
    else:
        q_ref, kn_ref, vn_ref, qg_ref, kg_ref, o_ref, ko_ref, vo_ref, k_scr, v_scr = refs
    qi = pl.program_id(2)

    @pl.when(qi == 0)
    def _():
        k = _rms(kn_ref[...].astype(F32)) * kg_ref[...]
        if latent:
            k = _rope(k, cos_ref[...], sin_ref[...])
            k_scr[0:past, :] = ck_ref[...].astype(BF16)
            v_scr[0:past, :] = cv_ref[...].astype(BF16)
        else:
            ko_ref[...] = k
            vo_ref[...] = vn_ref[...].astype(F32)
        k_scr[past:, :] = k.astype(BF16)
        v_scr[past:, :] = vn_ref[...]

    q = q_ref[...].astype(F32)
    heads = []
    for h in range(grp):
        qh = _rms(q[:, h * LANE:(h + 1) * LANE]) * qg_ref[...]
        if latent:
            r0 = pl.multiple_of(qi * tq, tq)
            qh = _rope(qh, cos_ref[pl.ds(r0, tq), :], sin_ref[pl.ds(r0, tq), :])
        heads.append((qh * scale).astype(BF16))
    qs = jnp.concatenate(heads, axis=0)
    s = lax.dot_general(qs, k_scr[...], NT_DIMS, preferred_element_type=F32)
    p = jnp.exp2(s - jnp.max(s, axis=-1, keepdims=True)).astype(BF16)
    denom = jnp.dot(p, jnp.ones((p.shape[1], LANE), BF16), preferred_element_type=F32)
    o = jnp.dot(p, v_scr[...], preferred_element_type=F32) / denom
    for h in range(grp):
        o_ref[:, h * LANE:(h + 1) * LANE] = o[h * tq:(h + 1) * tq, :].astype(BF16)


def _attention(p_all, col, row0, n_seq, seq_len, n_heads, kvh, q_gain, k_gain, tq, ctx=None):
    grp = n_heads // kvh
    latent = ctx is not None
    qw = grp * LANE
    assert col["b_q"] % qw == 0 and row0 % seq_len == 0 and seq_len % tq == 0
    seq0, nq = row0 // seq_len, seq_len // tq
    q_spec = pl.BlockSpec((tq, qw), lambda b, g, i: (seq0 * nq + b * nq + i, col["b_q"] // qw + g))
    k_spec = pl.BlockSpec((seq_len, LANE), lambda b, g, i: (seq0 + b, col["b_k"] // LANE + g))
    v_spec = pl.BlockSpec((seq_len, LANE), lambda b, g, i: (seq0 + b, col["b_v"] // LANE + g))
    gain_spec = pl.BlockSpec((1, LANE), lambda b, g, i: (0, 0))
    in_specs = [q_spec, k_spec, v_spec]
    args = [p_all, p_all, p_all]
    past = 0
    if latent:
        cache_k, cache_v, layer, cos, sin = ctx
        past = cache_k.shape[2]
        c_spec = pl.BlockSpec((None, None, past, LANE), lambda b, g, i: (b, layer, 0, g))
        t_spec = pl.BlockSpec((seq_len, LANE), lambda b, g, i: (0, 0))
        in_specs += [c_spec, c_spec, t_spec, t_spec]
        args += [cache_k, cache_v, cos, sin]
    in_specs += [gain_spec, gain_spec]
    args += [q_gain.reshape(1, LANE), k_gain.reshape(1, LANE)]
    out_shape = [jax.ShapeDtypeStruct((n_seq * seq_len, n_heads * LANE), BF16)]
    out_specs = [pl.BlockSpec((tq, qw), lambda b, g, i: (b * nq + i, g))]
    if not latent:
        kv_shape = jax.ShapeDtypeStruct((n_seq, seq_len, kvh * LANE), F32)
        kv_spec = pl.BlockSpec((None, seq_len, LANE), lambda b, g, i: (b, 0, g))
        out_shape += [kv_shape, kv_shape]
        out_specs += [kv_spec, kv_spec]
    return pl.pallas_call(
        functools.partial(_attn_kernel, latent=latent, past=past, tq=tq, grp=grp, scale=LANE ** -0.5 * LOG2E),
        out_shape=out_shape,
        grid=(n_seq, kvh, nq),
        in_specs=in_specs,
        out_specs=out_specs,
        scratch_shapes=[pltpu.VMEM((past + seq_len, LANE), BF16), pltpu.VMEM((past + seq_len, LANE), BF16)],
        compiler_params=_params(("parallel", "parallel", "arbitrary")),
        name="attention_latent" if latent else "attention_context",
    )(*args)


def _rope_tables(seq_len):
    quarter = LANE // 4
    inv = ROPE_THETA ** (-jnp.arange(quarter, dtype=F32) / quarter)
    pos = jnp.arange(seq_len)
    row = (pos // GRID_W).astype(F32)[:, None] * inv[None, :]
    colp = (pos % GRID_W).astype(F32)[:, None] * inv[None, :]
    cos = jnp.concatenate([jnp.cos(row), jnp.cos(row), jnp.cos(colp), jnp.cos(colp)], axis=-1)
    sin = jnp.concatenate([-jnp.sin(row), jnp.sin(row), -jnp.sin(colp), jnp.sin(colp)], axis=-1)
    return cos, sin


def _hgrn_kernel(q_ref, ff_ref, fb_ref, i_ref, g_ref, lb_ref, ng_ref, s0_ref, y_ref, sfin_ref,
                 of_scr, ob_scr, sf_scr, sb_scr, *, seq_len):
    sb, half, blk = SUB_BLOCK, SUB_BLOCK // 2, HGRN_BLOCK
    n_sub, n_blk = blk // sb, seq_len // blk
    r_id = lax.broadcasted_iota(jnp.int32, (blk, blk), 0)
    c_id = lax.broadcasted_iota(jnp.int32, (blk, blk), 1)
    same = (r_id // sb) == (c_id // sb)
    prefix = (same & (r_id >= c_id)).astype(BF16)
    suffix = (same & (r_id <= c_id)).astype(BF16)
    total = same.astype(BF16)
    ones = jnp.ones((LANE, LANE), BF16)
    t_id = lax.broadcasted_iota(jnp.int32, (half, LANE), 0)

    sf_scr[...] = s0_ref[0].T
    sb_scr[...] = s0_ref[1].T

    def sums(tri, parts):
        return sum(jnp.dot(tri, p, preferred_element_type=F32) for p in parts)

    def direction(r0, f_ref, lb, st_scr, o_scr, tri, forward):
        q = _silu(q_ref[pl.ds(r0, blk), :].astype(F32))
        g = lb + (1.0 - lb) * _sigmoid(f_ref[pl.ds(r0, blk), :].astype(F32))
        k = 1.0 - g
        v = i_ref[pl.ds(r0, blk), :]
        vf = v.astype(F32)
        parts = _split3(jnp.log(g) * LOG2E)
        c = sums(tri, parts)
        c_end = sums(total, parts)
        qd = (q * jnp.exp2(c)).astype(BF16)
        kd = (k * jnp.exp2(c_end - c)).astype(BF16)
        dec = jnp.exp2(c_end)
        st = st_scr[...]
        for j in (range(n_sub) if forward else reversed(range(n_sub))):
            rs = slice(j * sb, (j + 1) * sb)
            qj, kj, cj, vj = q[rs], k[rs], c[rs], vf[rs]
            o = lax.dot_general(qd[rs], st.astype(BF16), NT_DIMS, preferred_element_type=F32)
            tiles, where_to = [], []
            for s in range(sb):
                for th in range(2):
                    lo_t = th * half
                    dead = (lo_t + half - 1 < s) if forward else (lo_t > s)
                    if dead:
                        continue
                    ts = slice(lo_t, lo_t + half)
                    w = jnp.exp2(cj[ts] - cj[s:s + 1, :])
                    full = (lo_t >= s) if forward else (lo_t + half - 1 <= s)
                    if not full:
                        live = (t_id + lo_t >= s) if forward else (t_id + lo_t <= s)
                        w = jnp.where(live, w, 0.0)
                    tiles.append(qj[ts] * kj[s:s + 1, :] * w)
                    where_to.append((th, s))
            score = jnp.dot(jnp.concatenate(tiles, axis=0).astype(BF16), ones, preferred_element_type=F32)
            acc = [o[0:half], o[half:sb]]
            for n, (th, s) in enumerate(where_to):
                acc[th] = acc[th] + score[n * half:(n + 1) * half, :] * vj[s:s + 1, :]
            o_scr[pl.ds(r0 + j * sb, half), :] = acc[0]
            o_scr[pl.ds(r0 + j * sb + half, half), :] = acc[1]
            st = st * dec[j * sb:j * sb + 1, :] + lax.dot_general(v[rs], kd[rs], TN_DIMS,
                                                                  preferred_element_type=F32)
        st_scr[...] = st

    lb_f, lb_b = lb_ref[0:1, :], lb_ref[1:2, :]

    def body(j, carry):
        direction(pl.multiple_of(j * blk, blk), ff_ref, lb_f, sf_scr, of_scr, prefix, True)
        direction(pl.multiple_of((n_blk - 1 - j) * blk, blk), fb_ref, lb_b, sb_scr, ob_scr, suffix, False)
        return carry

    lax.fori_loop(0, n_blk, body, 0)
    sfin_ref[0] = sf_scr[...].T
    sfin_ref[1] = sb_scr[...].T
    o = _rms(of_scr[...] + ob_scr[...]) * ng_ref[...]
    y_ref[...] = (o * _silu(g_ref[...].astype(F32))).astype(BF16)


def _hgrn(p_all, col, row0, n_seq, seq_len, n_heads, lb_l, norm_g, s0):
    assert row0 % seq_len == 0 and seq_len % HGRN_BLOCK == 0
    seq0 = row0 // seq_len

    def cspec(name):
        return pl.BlockSpec((seq_len, LANE), lambda b, h: (seq0 + b, col[name] // LANE + h))

    st_spec = pl.BlockSpec((None, 2, None, LANE, LANE), lambda b, h: (b, 0, h, 0, 0))
    return pl.pallas_call(
        functools.partial(_hgrn_kernel, seq_len=seq_len),
        out_shape=[jax.ShapeDtypeStruct((n_seq * seq_len, n_heads * LANE), BF16),
                   jax.ShapeDtypeStruct(s0.shape, F32)],
        grid=(n_seq, n_heads),
        in_specs=[cspec("a_q"), cspec("a_ff"), cspec("a_fb"), cspec("a_i"), cspec("a_g"),
                  pl.BlockSpec((2, LANE), lambda b, h: (0, h)),
                  pl.BlockSpec((1, LANE), lambda b, h: (0, 0)),
                  st_spec],
        out_specs=[pl.BlockSpec((seq_len, LANE), lambda b, h: (b, h)), st_spec],
        scratch_shapes=[pltpu.VMEM((seq_len, LANE), F32), pltpu.VMEM((seq_len, LANE), F32),
                        pltpu.VMEM((LANE, LANE), F32), pltpu.VMEM((LANE, LANE), F32)],
        compiler_params=_params(("parallel", "parallel")),
        name="hgrn_%d" % seq_len,
    )(p_all, p_all, p_all, p_all, p_all, lb_l, norm_g.reshape(1, LANE), s0)


def _gmlp_kernel(u_ref, v_ref, gn_ref, ws_ref, bs_ref, o_ref, *, n_chunks, groups):
    u = _gelu_tanh(u_ref[...].astype(F32))
    v = (_rms(_gelu_tanh(v_ref[...].astype(F32))) * gn_ref[...]).astype(BF16)
    for n in range(n_chunks):
        rs = slice(n * MIX_CHUNK, (n + 1) * MIX_CHUNK)
        for g in range(groups):
            cs = slice(g * LANE, (g + 1) * LANE)
            mixed = jnp.dot(ws_ref[g], v[rs, cs], preferred_element_type=F32) + bs_ref[g]
            o_ref[rs, cs] = (u[rs, cs] * mixed).astype(BF16)


def _gmlp(p_all, col, norm_g, ws_bf, bias, tm):
    t = p_all.shape[0]
    groups = ws_bf.shape[0]
    cw = groups * LANE
    assert col["c_u"] % cw == 0 and col["c_v"] % cw == 0
    return pl.pallas_call(
        functools.partial(_gmlp_kernel, n_chunks=tm // MIX_CHUNK, groups=groups),
        out_shape=jax.ShapeDtypeStruct((t, cw), BF16),
        grid=(t // tm,),
        in_specs=[pl.BlockSpec((tm, cw), lambda i: (i, col["c_u"] // cw)),
                  pl.BlockSpec((tm, cw), lambda i: (i, col["c_v"] // cw)),
                  pl.BlockSpec((1, cw), lambda i: (0, 0)),
                  pl.BlockSpec((groups, MIX_CHUNK, MIX_CHUNK), lambda i: (0, 0, 0)),
                  pl.BlockSpec((groups, MIX_CHUNK, LANE), lambda i: (0, 0, 0))],
        out_specs=pl.BlockSpec((tm, cw), lambda i: (i, 0)),
        compiler_params=_params(("parallel",)),
        name="gmlp",
    )(p_all, p_all, norm_g.reshape(1, cw), ws_bf, bias)


def _route_rows(logit_rows):
    per = len(logit_rows) // N_GROUPS
    m = functools.reduce(jnp.maximum, logit_rows)
    p = [jnp.exp(r - m) for r in logit_rows]

    def top2_sum(a):
        pairs = [a[i] + a[j] for i in range(len(a)) for j in range(i + 1, len(a))]
        return functools.reduce(jnp.maximum, pairs)

    scores = [top2_sum(p[g * per:(g + 1) * per]) for g in range(N_GROUPS)]
    best, gi = scores[0], jnp.zeros(m.shape, jnp.int32)
    for g in range(1, N_GROUPS):
        better = scores[g] > best
        gi = jnp.where(better, g, gi)
        best = jnp.where(better, scores[g], best)
    a = []
    for j in range(per):
        aj = p[(N_GROUPS - 1) * per + j]
        for g in reversed(range(N_GROUPS - 1)):
            aj = jnp.where(gi == g, p[g * per + j], aj)
        a.append(aj)
    v1, i1 = a[0], jnp.zeros(m.shape, jnp.int32)
    for j in range(1, per):
        better = a[j] > v1
        i1 = jnp.where(better, j, i1)
        v1 = jnp.where(better, a[j], v1)
    v2, i2 = jnp.full(m.shape, -1.0, F32), jnp.zeros(m.shape, jnp.int32)
    for j in range(per):
        better = (i1 != j) & (a[j] > v2)
        i2 = jnp.where(better, j, i2)
        v2 = jnp.where(better, a[j], v2)
    inv = 1.0 / (v1 + v2)
    return gi * per + i1, gi * per + i2, v1 * inv, v2 * inv


def _merge_kernel(x_ref, ya_ref, yb_ref, yc_ref, ga0, ga1, gb0, gb1, gc0, gc1, wpa_ref, wpb_ref, wpc_ref,
                  wout_ref, g1_ref, n2_ref, sh2_ref, sc2_ref, rhi_ref, rlo_ref, rb_ref, x1_ref, h2_ref,
                  rt_ref, cnt_ref, *, n_experts):
    def gate(r0, r1):
        return jnp.concatenate([_sigmoid(r0[...].astype(F32)), _sigmoid(r1[...].astype(F32))], axis=-1)

    merged = gate(ga0, ga1) * jnp.dot(ya_ref[...], wpa_ref[...], preferred_element_type=F32)
    merged += gate(gb0, gb1) * jnp.dot(yb_ref[...], wpb_ref[...], preferred_element_type=F32)
    merged += gate(gc0, gc1) * jnp.dot(yc_ref[...], wpc_ref[...], preferred_element_type=F32)
    out = jnp.dot(merged.astype(BF16), wout_ref[...], preferred_element_type=F32)
    x1 = x_ref[...] + g1_ref[...] * out
    x1_ref[...] = x1
    h2 = _rms(x1) * n2_ref[...] * (1.0 + sc2_ref[...]) + sh2_ref[...]
    h2_ref[...] = h2.astype(BF16)
    hi, lo = _split2(h2)
    rhi, rlo = rhi_ref[...], rlo_ref[...]
    logits = (jnp.dot(hi, rhi, preferred_element_type=F32) + jnp.dot(hi, rlo, preferred_element_type=F32)
              + jnp.dot(lo, rhi, preferred_element_type=F32))
    lt = logits.T[0:n_experts, :] + rb_ref[...]
    tm = lt.shape[1]
    e1, e2, w1, w2 = _route_rows([lt[e:e + 1, :] for e in range(n_experts)])

    @pl.when(pl.program_id(0) == 0)
    def _():
        cnt_ref[...] = jnp.zeros_like(cnt_ref)

    e_id = lax.broadcasted_iota(jnp.int32, (n_experts, tm), 0)
    hit1, hit2 = e_id == e1, e_id == e2
    onehot = (hit1 | hit2).astype(BF16)
    src = lax.broadcasted_iota(jnp.int32, (tm, tm), 0)
    dst = lax.broadcasted_iota(jnp.int32, (tm, tm), 1)
    before = jnp.dot(onehot, (src < dst).astype(BF16), preferred_element_type=F32) + cnt_ref[...]
    rank1 = jnp.sum(jnp.where(hit1, before, 0.0), axis=0, keepdims=True)
    rank2 = jnp.sum(jnp.where(hit2, before, 0.0), axis=0, keepdims=True)
    cnt_ref[...] += jnp.dot(onehot, jnp.ones((tm, tm), BF16), preferred_element_type=F32)
    zero = jnp.zeros_like(w1)
    rt_ref[...] = jnp.concatenate([e1.astype(F32), e2.astype(F32), w1, w2, rank1, rank2, zero, zero], axis=0)


ROUTE_ROWS = 8


def _merge(rows, x, y_a, y_b, y_c, p_all, col, mod_l, norm2_g, w_pa, w_pb, w_pc, w_out, r_hi, r_lo, r_bias,
           tm):
    t, d = x.shape
    n_experts = r_bias.shape[0]
    gw = d // 2
    assert col["gates"] % gw == 0
    g0 = col["gates"] // gw
    row = lambda w: pl.BlockSpec((tm, w), lambda i: (i, 0))
    full = lambda a: pl.BlockSpec(a.shape, lambda i: (0,) * a.ndim, pipeline_mode=pl.Buffered(1))
    gate_specs = [pl.BlockSpec((tm, gw), functools.partial(lambda i, k: (i, g0 + k), k=k)) for k in range(6)]
    return pl.pallas_call(
        functools.partial(_merge_kernel, n_experts=n_experts),
        out_shape=[jax.ShapeDtypeStruct((t, d), F32), jax.ShapeDtypeStruct((t, d), BF16),
                   jax.ShapeDtypeStruct((ROUTE_ROWS, t), F32), jax.ShapeDtypeStruct((n_experts, tm), F32)],
        grid=(t // tm,),
        in_specs=[row(d), row(y_a.shape[1]), row(y_b.shape[1]), row(y_c.shape[1])] + gate_specs
                 + [full(w_pa), full(w_pb), full(w_pc), full(w_out), rows.mod_spec(2, tm, d),
                    pl.BlockSpec((1, d), lambda i: (0, 0)), rows.mod_spec(3, tm, d), rows.mod_spec(4, tm, d),
                    full(r_hi), full(r_lo), pl.BlockSpec((n_experts, tm), lambda i: (0, 0))],
        out_specs=[row(d), row(d), pl.BlockSpec((ROUTE_ROWS, tm), lambda i: (0, i)),
                   pl.BlockSpec((n_experts, tm), lambda i: (0, 0))],
        compiler_params=_params(("arbitrary",)),
        name="merge_out_proj",
    )(x, y_a, y_b, y_c, *([p_all] * 6), w_pa, w_pb, w_pc, w_out, mod_l, norm2_g.reshape(1, d), mod_l, mod_l,
      r_hi, r_lo, jnp.broadcast_to(r_bias.astype(F32)[:, None], (n_experts, tm)))


def _moe_kernel(be_ref, bv_ref, x_ref, wg_ref, wu_ref, wd_ref, o_ref):
    i = pl.program_id(0)

    @pl.when(bv_ref[i] > 0)
    def _():
        x = x_ref[...]
        a = jnp.dot(x, wg_ref[...], preferred_element_type=F32)
        b = jnp.dot(x, wu_ref[...], preferred_element_type=F32)
        h = (_silu(a) * b).astype(BF16)
        o_ref[...] = jnp.dot(h, wd_ref[...], preferred_element_type=F32).astype(BF16)

    @pl.when(bv_ref[i] == 0)
    def _():
        o_ref[...] = jnp.zeros_like(o_ref)


def _moe_ffn(xb, block_e, block_valid, wg, wu, wd, tm):
    n_slots, d = xb.shape
    ff = wg.shape[2]
    grid_spec = pltpu.PrefetchScalarGridSpec(
        num_scalar_prefetch=2,
        grid=(n_slots // tm,),
        in_specs=[pl.BlockSpec((tm, d), lambda i, be, bv: (i, 0)),
                  pl.BlockSpec((None, d, ff), lambda i, be, bv: (be[i], 0, 0)),
                  pl.BlockSpec((None, d, ff), lambda i, be, bv: (be[i], 0, 0)),
                  pl.BlockSpec((None, ff, d), lambda i, be, bv: (be[i], 0, 0))],
        out_specs=pl.BlockSpec((tm, d), lambda i, be, bv: (i, 0)),
    )
    return pl.pallas_call(
        _moe_kernel,
        out_shape=jax.ShapeDtypeStruct((n_slots, d), BF16),
        grid_spec=grid_spec,
        compiler_params=_params(("arbitrary",)),
        name="moe_experts",
    )(block_e, block_valid, xb, wg, wu, wd)


def _dispatch(expert, rank, counts, tm):
    n_experts = counts.shape[0]
    n_tok = expert.shape[1]
    n_assign = TOP_K * n_tok
    padded = (counts + tm - 1) // tm * tm
    pad_end = jnp.cumsum(padded)
    pad_start = pad_end - padded
    dest = jnp.take(pad_start, expert) + rank
    n_blocks = -(-(n_assign + n_experts * (tm - 1)) // tm)
    tok = jnp.broadcast_to(jnp.arange(n_tok, dtype=jnp.int32)[None, :], dest.shape)
    slot_tok = jnp.zeros((n_blocks * tm,), jnp.int32).at[dest.reshape(-1)].set(tok.reshape(-1))
    block_start = jnp.arange(n_blocks, dtype=jnp.int32) * tm
    block_e = jnp.minimum(jnp.sum(block_start[:, None] >= pad_end[None, :], axis=1), n_experts - 1)
    block_valid = (block_start < pad_end[-1]).astype(jnp.int32)
    return slot_tok, dest, block_e.astype(jnp.int32), block_valid


def _final_kernel(x_ref, moe_ref, g2_ref, nf_ref, o_ref):
    x = x_ref[...] + g2_ref[...] * moe_ref[...].astype(F32)
    o_ref[...] = _rms(x) * nf_ref[...]


def _final_norm(rows, x, moe, mod_l, norm_f, row0, n_rows, tm):
    d = x.shape[1]
    t0 = row0 // tm
    row_spec = pl.BlockSpec((tm, d), lambda i: (t0 + i, 0))
    g2_spec = pl.BlockSpec((None, 1, d), lambda i: (rows.cond_row(t0 + i, tm) * N_MOD + 5, 0, 0))
    return pl.pallas_call(
        _final_kernel,
        out_shape=jax.ShapeDtypeStruct((n_rows, d), F32),
        grid=(n_rows // tm,),
        in_specs=[row_spec, row_spec, g2_spec, pl.BlockSpec((1, d), lambda i: (0, 0))],
        out_specs=pl.BlockSpec((tm, d), lambda i: (i, 0)),
        compiler_params=_params(("parallel",)),
        name="final_norm",
    )(x, moe, mod_l, norm_f.reshape(1, d))


def kernel(x_prompt, x_sample, cache_k, cache_v, state_hgrn, c, c_ctx, w_mod, b_mod, norm1, norm2, norm_f,
           w_in, hgrn_lb, hgrn_norm, q_norm, k_norm, gmlp_norm, gmlp_ws, gmlp_bs, w_pa, w_pb, w_pc, w_out,
           router_w, router_b, exp_gate, exp_up, exp_down):
    n_ctx_seq, ctx_len, d = x_prompt.shape
    n_lat_seq, lat_len, _ = x_sample.shape
    depth = w_mod.shape[0]
    past, kvh = cache_k.shape[2], cache_k.shape[3]
    a_heads = state_hgrn.shape[3]
    a_width, b_width, c_width = hgrn_lb.shape[2], w_pb.shape[1], w_pc.shape[1]
    b_heads = b_width // LANE
    n_experts = router_w.shape[1]
    assert cache_k.shape[4] == LANE and state_hgrn.shape[4:] == (LANE, LANE) and a_width == a_heads * LANE

    sizes = [("a_q", a_width), ("a_ff", a_width), ("a_fb", a_width), ("a_i", a_width), ("a_g", a_width),
             ("b_q", b_width), ("b_k", kvh * LANE), ("b_v", kvh * LANE), ("c_u", c_width), ("c_v", c_width),
             ("gates", 3 * d)]
    col, acc = {}, 0
    for name, width in sizes:
        col[name] = acc
        acc += width
    assert acc == w_in.shape[2]

    n_ctx = n_ctx_seq * ctx_len
    rows = _Rows(n_ctx, n_lat_seq, lat_len)
    tm_proj = min(1024, n_ctx, lat_len)
    tn_proj = min(1024, math.gcd(w_in.shape[2], 1024))
    tm_row = min(256, n_ctx, lat_len)
    tm_moe = 256
    tq_lat = min(128, lat_len)

    cond = jnp.concatenate([c_ctx[None, :], c, jnp.zeros((COND_ROWS - 1 - n_lat_seq, d), F32)], axis=0)
    mod = _modulation(cond, w_mod, b_mod).reshape(depth, COND_ROWS * N_MOD, 1, d)

    p_lb = jax.nn.softmax(hgrn_lb.astype(F32), axis=1)
    lbs = jnp.cumsum(p_lb, axis=1) - p_lb[:, :1]

    cos, sin = _rope_tables(lat_len)
    cache_k2 = cache_k.reshape(n_lat_seq, depth, past, kvh * LANE)
    cache_v2 = cache_v.reshape(n_lat_seq, depth, past, kvh * LANE)
    s0_ctx = jnp.zeros((n_ctx_seq,) + state_hgrn.shape[2:], F32)
    r_pad = jnp.zeros((d, LANE), F32).at[:, :n_experts].set(router_w)
    r_hi, r_lo = _split2(r_pad)
    gmlp_bias = jnp.broadcast_to(gmlp_bs[:, :, :, None], gmlp_bs.shape + (LANE,))

    x = jnp.concatenate([x_prompt.reshape(n_ctx, d), x_sample.reshape(n_lat_seq * lat_len, d)], axis=0)
    moe = None
    new_k, new_v, new_s = [], [], []
    for l in range(depth):
        mod_l = mod[l]
        p_all, x = _in_proj(rows, x, moe, mod[l - 1] if l else None, mod_l, norm1[l], w_in[l].astype(BF16),
                            tm_proj, tn_proj)
        yb_ctx, k_l, v_l = _attention(p_all, col, 0, n_ctx_seq, ctx_len, b_heads, kvh, q_norm[l], k_norm[l],
                                      ctx_len)
        (yb_lat,) = _attention(p_all, col, n_ctx, n_lat_seq, lat_len, b_heads, kvh, q_norm[l], k_norm[l],
                               tq_lat, ctx=(cache_k2, cache_v2, l, cos, sin))
        ya_ctx, s_l = _hgrn(p_all, col, 0, n_ctx_seq, ctx_len, a_heads, lbs[:, l], hgrn_norm[l], s0_ctx)
        ya_lat, _ = _hgrn(p_all, col, n_ctx, n_lat_seq, lat_len, a_heads, lbs[:, l], hgrn_norm[l],
                          state_hgrn[:, l])
        y_c = _gmlp(p_all, col, gmlp_norm[l], gmlp_ws[l].astype(BF16), gmlp_bias[l], tm_proj)
        x, h2, route, counts = _merge(rows, x, jnp.concatenate([ya_ctx, ya_lat], axis=0),
                                      jnp.concatenate([yb_ctx, yb_lat], axis=0), y_c, p_all, col, mod_l,
                                      norm2[l], w_pa[l].astype(BF16), w_pb[l].astype(BF16),
                                      w_pc[l].astype(BF16), w_out[l].astype(BF16), r_hi, r_lo, router_b, tm_row)
        slot_tok, dest, block_e, block_valid = _dispatch(
            route[0:2].astype(jnp.int32), route[4:6].astype(jnp.int32), counts[:, 0].astype(jnp.int32), tm_moe)
        yb = _moe_ffn(jnp.take(h2, slot_tok, axis=0), block_e, block_valid, exp_gate[l].astype(BF16),
                      exp_up[l].astype(BF16), exp_down[l].astype(BF16), tm_moe)
        moe = (jnp.take(yb, dest[0], axis=0).astype(F32) * route[2][:, None]
               + jnp.take(yb, dest[1], axis=0).astype(F32) * route[3][:, None]).astype(BF16)
        new_k.append(k_l)
        new_v.append(v_l)
        new_s.append(s_l)

    y_ctx = _final_norm(rows, x, moe, mod[depth - 1], norm_f, 0, n_ctx, tm_row)
    y_lat = _final_norm(rows, x, moe, mod[depth - 1], norm_f, n_ctx, n_lat_seq * lat_len, tm_row)
    kv_shape = (n_ctx_seq, depth, ctx_len, kvh, LANE)
    return (y_ctx.reshape(n_ctx_seq, ctx_len, d), y_lat.reshape(n_lat_seq, lat_len, d),
            jnp.stack(new_k, axis=1).reshape(kv_shape), jnp.stack(new_v, axis=1).reshape(kv_shape),
            jnp.stack(new_s, axis=1))
```

```python
import functools
import math

import jax
import jax.numpy as jnp
import numpy as np
from jax import lax
from jax.experimental import pallas as pl
from jax.experimental.pallas import tpu as pltpu

F32 = jnp.float32
BF16 = jnp.bfloat16

EPS = 1e-6
N_MOD = 6
GRID_W = 64
ROPE_THETA = 10000.0
MIX_CHUNK = 128
N_GROUPS = 4
TOP_K = 2
LANE = 128
HGRN_BLOCK = 128
HGRN_LEVELS = 7
ATTN_KEY_CHUNK = 1024
LOG2E = 1.4426950408889634
COND_ROWS = 16
VMEM_LIMIT = 56 * 1024 * 1024

NT_DIMS = (((1,), (1,)), ((), ()))
TN_DIMS = (((0,), (0,)), ((), ()))


def _sigmoid(x):
    return 1.0 / (1.0 + jnp.exp(-x))


def _silu(x):
    return x * _sigmoid(x)


def _gelu_tanh(x):
    return 0.5 * x * (1.0 + jnp.tanh(math.sqrt(2.0 / math.pi) * (x + 0.044715 * (x * x * x))))


def _rms(x):
    return x * lax.rsqrt(jnp.mean(x * x, axis=-1, keepdims=True) + EPS)


def _split2(x):
    hi = x.astype(BF16)
    lo = (x - hi.astype(F32)).astype(BF16)
    return hi, lo


def _split3(x):
    hi = x.astype(BF16)
    r = x - hi.astype(F32)
    mid = r.astype(BF16)
    lo = (r - mid.astype(F32)).astype(BF16)
    return hi, mid, lo


def _params(sem):
    return pltpu.CompilerParams(dimension_semantics=sem, vmem_limit_bytes=VMEM_LIMIT)


def _mod_kernel(c_ref, w_ref, b_ref, o_ref):
    s = _silu(c_ref[...]).astype(BF16)
    o_ref[...] = jnp.dot(s, w_ref[...].astype(BF16), preferred_element_type=F32) + b_ref[...]


def _modulation(cond, w_mod, b_mod):
    depth, d, n = w_mod.shape
    tn = math.gcd(n, 1024)
    return pl.pallas_call(
        _mod_kernel,
        out_shape=jax.ShapeDtypeStruct((depth, COND_ROWS, n), F32),
        grid=(depth, n // tn),
        in_specs=[
            pl.BlockSpec((COND_ROWS, d), lambda l, j: (0, 0)),
            pl.BlockSpec((None, d, tn), lambda l, j: (l, 0, j)),
            pl.BlockSpec((None, 1, tn), lambda l, j: (l, 0, j)),
        ],
        out_specs=pl.BlockSpec((None, COND_ROWS, tn), lambda l, j: (l, 0, j)),
        compiler_params=_params(("parallel", "parallel")),
        name="adaln_modulation",
    )(cond, w_mod, b_mod.reshape(depth, 1, n))


class _Rows:
    def __init__(self, n_ctx, n_lat_seq, lat_len):
        self.n_ctx, self.lat_len = n_ctx, lat_len
        self.total = n_ctx + n_lat_seq * lat_len

    def cond_row(self, i, tm):
        ctx_tiles = self.n_ctx // tm
        per_seq = self.lat_len // tm
        return jnp.where(i < ctx_tiles, 0, 1 + (i - ctx_tiles) // per_seq)

    def mod_spec(self, k, tm, d):
        return pl.BlockSpec((None, 1, d), lambda i, *_: (self.cond_row(i, tm) * N_MOD + k, 0, 0))


def _in_proj_kernel(*refs, combine):
    if combine:
        x_ref, moe_ref, g2_ref, n_ref, sh_ref, sc_ref, w_ref, o_ref, x2_ref, h_scr = refs
    else:
        x_ref, n_ref, sh_ref, sc_ref, w_ref, o_ref, h_scr = refs

    @pl.when(pl.program_id(1) == 0)
    def _():
        x = x_ref[...]
        if combine:
            x = x + g2_ref[...] * moe_ref[...].astype(F32)
            x2_ref[...] = x
        h = _rms(x) * n_ref[...] * (1.0 + sc_ref[...]) + sh_ref[...]
        h_scr[...] = h.astype(BF16)

    o_ref[...] = jnp.dot(h_scr[...], w_ref[...], preferred_element_type=F32).astype(BF16)


def _in_proj(rows, x, moe, mod_prev, mod_l, norm_g, w_bf, tm, tn):
    t, d = x.shape
    n = w_bf.shape[1]
    combine = moe is not None
    row_spec = pl.BlockSpec((tm, d), lambda i, j: (i, 0))
    row_in_spec = pl.BlockSpec((tm, d), lambda i, j: (i, 0), pipeline_mode=pl.Buffered(1))
    vec_spec = pl.BlockSpec((1, d), lambda i, j: (0, 0))
    in_specs = [row_in_spec]
    args = [x]
    if combine:
        in_specs += [row_in_spec, rows.mod_spec(5, tm, d)]
        args += [moe, mod_prev]
    in_specs += [vec_spec, rows.mod_spec(0, tm, d), rows.mod_spec(1, tm, d),
                 pl.BlockSpec((d, tn), lambda i, j: (0, j))]
    args += [norm_g.reshape(1, d), mod_l, mod_l, w_bf]
    out_shape = [jax.ShapeDtypeStruct((t, n), BF16)]
    out_specs = [pl.BlockSpec((tm, tn), lambda i, j: (i, j))]
    if combine:
        out_shape.append(jax.ShapeDtypeStruct((t, d), F32))
        out_specs.append(row_spec)
    res = pl.pallas_call(
        functools.partial(_in_proj_kernel, combine=combine),
        out_shape=out_shape,
        grid=(t // tm, n // tn),
        in_specs=in_specs,
        out_specs=out_specs,
        scratch_shapes=[pltpu.VMEM((tm, d), BF16)],
        compiler_params=_params(("parallel", "arbitrary")),
        name="in_proj",
    )(*args)
    return (res[0], res[1]) if combine else (res[0], x)


def _rope(x, cos, sin_signed):
    lane = lax.broadcasted_iota(jnp.int32, x.shape, 1)
    first = (lane // (LANE // 4)) % 2 == 0
    swapped = jnp.where(first, pltpu.roll(x, LANE - LANE // 4, 1), pltpu.roll(x, LANE // 4, 1))
    return x * cos + swapped * sin_signed


def _attn_kernel(*refs, latent, past, tq, grp, scale):
    if latent:
        (q_ref, kn_ref, vn_ref, ck_ref, cv_ref, cos_ref, sin_ref, qg_ref, kg_ref,
         o_ref, k_scr, v_scr) = refs
    else:
        q_ref, kn_ref, vn_ref, qg_ref, kg_ref, o_ref, ko_ref, vo_ref, k_scr, v_scr = refs
    qi = pl.program_id(2)

    @pl.when(qi == 0)
    def _():
        k = _rms(kn_ref[...].astype(F32)) * kg_ref[...]
        if latent:
            k = _rope(k, cos_ref[...], sin_ref[...])
            k_scr[0:past, :] = ck_ref[...].astype(BF16)
            v_scr[0:past, 0:LANE] = cv_ref[...].astype(BF16)
        else:
            ko_ref[...] = k
            vo_ref[...] = vn_ref[...].astype(F32)
        k_scr[past:, :] = k.astype(BF16)
        v_scr[past:, 0:LANE] = vn_ref[...]
        v_scr[:, LANE:] = jnp.ones((v_scr.shape[0], LANE), BF16)

    q = q_ref[...].astype(F32)
    heads = []
    for h in range(grp):
        qh = _rms(q[:, h * LANE:(h + 1) * LANE]) * qg_ref[...]
        if latent:
            r0 = pl.multiple_of(qi * tq, tq)
            qh = _rope(qh, cos_ref[pl.ds(r0, tq), :], sin_ref[pl.ds(r0, tq), :])
        heads.append((qh * scale).astype(BF16))
    qs = jnp.concatenate(heads, axis=0)
    n_keys = k_scr.shape[0]
    m = acc = None
    for c0 in range(0, n_keys, ATTN_KEY_CHUNK):
        c1 = min(c0 + ATTN_KEY_CHUNK, n_keys)
        s = lax.dot_general(qs, k_scr[c0:c1, :], NT_DIMS, preferred_element_type=F32)
        m_c = jnp.max(functools.reduce(jnp.maximum, [s[:, t:t + LANE] for t in range(0, c1 - c0, LANE)]),
                      axis=-1, keepdims=True)
        m_new = m_c if m is None else jnp.maximum(m, m_c)
        pv = jnp.dot(jnp.exp2(s - m_new).astype(BF16), v_scr[c0:c1, :], preferred_element_type=F32)
        acc = pv if m is None else acc * jnp.exp2(m - m_new) + pv
        m = m_new
    o = acc[:, 0:LANE] / acc[:, LANE:]
    for h in range(grp):
        o_ref[:, h * LANE:(h + 1) * LANE] = o[h * tq:(h + 1) * tq, :].astype(BF16)


def _attention(p_all, col, row0, n_seq, seq_len, n_heads, kvh, q_gain, k_gain, tq, ctx=None):
    grp = n_heads // kvh
    latent = ctx is not None
    qw = grp * LANE
    assert col["b_q"] % qw == 0 and row0 % seq_len == 0 and seq_len % tq == 0
    seq0, nq = row0 // seq_len, seq_len // tq
    q_spec = pl.BlockSpec((tq, qw), lambda b, g, i: (seq0 * nq + b * nq + i, col["b_q"] // qw + g))
    k_spec = pl.BlockSpec((seq_len, LANE), lambda b, g, i: (seq0 + b, col["b_k"] // LANE + g))
    v_spec = pl.BlockSpec((seq_len, LANE), lambda b, g, i: (seq0 + b, col["b_v"] // LANE + g))
    gain_spec = pl.BlockSpec((1, LANE), lambda b, g, i: (0, 0))
    in_specs = [q_spec, k_spec, v_spec]
    args = [p_all, p_all, p_all]
    past = 0
    if latent:
        cache_k, cache_v, layer, cos, sin = ctx
        past = cache_k.shape[2]
        c_spec = pl.BlockSpec((None, None, past, LANE), lambda b, g, i: (b, layer, 0, g))
        t_spec = pl.BlockSpec((seq_len, LANE), lambda b, g, i: (0, 0))
        in_specs += [c_spec, c_spec, t_spec, t_spec]
        args += [cache_k, cache_v, cos, sin]
    in_specs += [gain_spec, gain_spec]
    args += [q_gain.reshape(1, LANE), k_gain.reshape(1, LANE)]
    out_shape = [jax.ShapeDtypeStruct((n_seq * seq_len, n_heads * LANE), BF16)]
    out_specs = [pl.BlockSpec((tq, qw), lambda b, g, i: (b * nq + i, g))]
    if not latent:
        kv_shape = jax.ShapeDtypeStruct((n_seq, seq_len, kvh * LANE), F32)
        kv_spec = pl.BlockSpec((None, seq_len, LANE), lambda b, g, i: (b, 0, g))
        out_shape += [kv_shape, kv_shape]
        out_specs += [kv_spec, kv_spec]
    return pl.pallas_call(
        functools.partial(_attn_kernel, latent=latent, past=past, tq=tq, grp=grp, scale=LANE ** -0.5 * LOG2E),
        out_shape=out_shape,
        grid=(n_seq, kvh, nq),
        in_specs=in_specs,
        out_specs=out_specs,
        scratch_shapes=[pltpu.VMEM((past + seq_len, LANE), BF16), pltpu.VMEM((past + seq_len, 2 * LANE), BF16)],
        compiler_params=_params(("parallel", "parallel", "arbitrary")),
        name="attention_latent" if latent else "attention_context",
    )(*args)


def _rope_tables(seq_len):
    quarter = LANE // 4
    inv = ROPE_THETA ** (-jnp.arange(quarter, dtype=F32) / quarter)
    pos = jnp.arange(seq_len)
    row = (pos // GRID_W).astype(F32)[:, None] * inv[None, :]
    colp = (pos % GRID_W).astype(F32)[:, None] * inv[None, :]
    cos = jnp.concatenate([jnp.cos(row), jnp.cos(row), jnp.cos(colp), jnp.cos(colp)], axis=-1)
    sin = jnp.concatenate([-jnp.sin(row), jnp.sin(row), -jnp.sin(colp), jnp.sin(colp)], axis=-1)
    return cos, sin


def _hgrn_tables():
    blk = HGRN_BLOCK
    r = np.arange(blk)[:, None]
    u = np.arange(blk)[None, :]
    fwd, bwd = [], []
    for query_side in (True, False):
        for m in range(1, HGRN_LEVELS + 1):
            seg = (r >> m) == (u >> m)
            fwd.append(seg & ((u <= r) if query_side else (u > r)))
            bwd.append(seg & ((u >= r) if query_side else (u < r)))
    diff = r ^ u
    code = np.where(diff == 0, 0, 1 + np.floor(np.log2(np.maximum(diff, 1))).astype(np.int32))
    tables = [jnp.asarray(np.concatenate(t, axis=0), BF16) for t in (fwd, bwd)]
    codes = [jnp.asarray(np.where(keep, code, -1), jnp.int32) for keep in (r >= u, r <= u)]
    return tables + codes


def _hgrn_kernel(q_ref, ff_ref, fb_ref, i_ref, g_ref, lb_ref, ng_ref, s0_ref, trif_ref, trib_ref, lvlf_ref,
                 lvlb_ref, y_ref, sfin_ref, of_scr, ob_scr, *, seq_len):
    blk, n_lvl = HGRN_BLOCK, HGRN_LEVELS
    n_blk = seq_len // blk

    def scan_blocks(chains):
        q, k, v, log_g = [], [], [], []
        for r0, f_ref, lb, _, _, _, _ in chains:
            q.append(_silu(q_ref[pl.ds(r0, blk), :].astype(F32)))
            g = lb + (1.0 - lb) * _sigmoid(f_ref[pl.ds(r0, blk), :].astype(F32))
            k.append(1.0 - g)
            v.append(i_ref[pl.ds(r0, blk), :])
            log_g.append(jnp.log(g) * LOG2E)
        parts = [_split2(x) for x in log_g]
        runs = [jnp.dot(c[4][...], p[0], preferred_element_type=F32) for c, p in zip(chains, parts)]
        runs = [r + jnp.dot(c[4][...], p[1], preferred_element_type=F32) for r, c, p in zip(runs, chains, parts)]
        q_run = [[lg] + [r[(m - 1) * blk:m * blk] for m in range(1, n_lvl + 1)] for lg, r in zip(log_g, runs)]
        k_run = [[None] + [r[(n_lvl + m - 1) * blk:(n_lvl + m) * blk] for m in range(1, n_lvl + 1)] for r in runs]
        kb = [x.astype(BF16) for x in k]
        level = [c[5][...] for c in chains]
        score = [jnp.where(lv == 0, lax.dot_general(qc.astype(BF16), kc, NT_DIMS, preferred_element_type=F32), 0.0)
                 for lv, qc, kc in zip(level, q, kb)]
        for m in range(n_lvl):
            for i in range(len(chains)):
                qs = (q[i] * jnp.exp2(q_run[i][m])).astype(BF16)
                ks = kb[i] if m == 0 else (k[i] * jnp.exp2(k_run[i][m])).astype(BF16)
                score[i] = jnp.where(level[i] == m + 1,
                                     lax.dot_general(qs, ks, NT_DIMS, preferred_element_type=F32), score[i])
        out = []
        for i, (_, _, _, st, _, _, edge) in enumerate(chains):
            q_all = jnp.exp2(q_run[i][n_lvl])
            o = jnp.dot(score[i].astype(BF16), v[i], preferred_element_type=F32)
            o += lax.dot_general((q[i] * q_all).astype(BF16), st.astype(BF16), NT_DIMS,
                                 preferred_element_type=F32)
            kd = (k[i] * jnp.exp2(k_run[i][n_lvl])).astype(BF16)
            st = st * q_all[edge:edge + 1, :] + lax.dot_general(v[i], kd, TN_DIMS, preferred_element_type=F32)
            out.append((o, st))
        return out

    lb_f, lb_b = lb_ref[0:1, :], lb_ref[1:2, :]

    def body(j, carry):
        st_f, st_b = carry
        r_f = pl.multiple_of(j * blk, blk)
        r_b = pl.multiple_of((n_blk - 1 - j) * blk, blk)
        (o_f, st_f), (o_b, st_b) = scan_blocks([(r_f, ff_ref, lb_f, st_f, trif_ref, lvlf_ref, blk - 1),
                                                (r_b, fb_ref, lb_b, st_b, trib_ref, lvlb_ref, 0)])
        of_scr[pl.ds(r_f, blk), :] = o_f
        ob_scr[pl.ds(r_b, blk), :] = o_b
        return st_f, st_b

    st_f, st_b = lax.fori_loop(0, n_blk, body, (s0_ref[0].T, s0_ref[1].T))
    sfin_ref[0] = st_f.T
    sfin_ref[1] = st_b.T
    o = _rms(of_scr[...] + ob_scr[...]) * ng_ref[...]
    y_ref[...] = (o * _silu(g_ref[...].astype(F32))).astype(BF16)


def _hgrn(p_all, col, row0, n_seq, seq_len, n_heads, lb_l, norm_g, s0):
    assert row0 % seq_len == 0 and seq_len % HGRN_BLOCK == 0
    seq0 = row0 // seq_len

    def cspec(name):
        return pl.BlockSpec((seq_len, LANE), lambda b, h: (seq0 + b, col[name] // LANE + h))

    st_spec = pl.BlockSpec((None, 2, None, LANE, LANE), lambda b, h: (b, 0, h, 0, 0))
    tables = _hgrn_tables()
    return pl.pallas_call(
        functools.partial(_hgrn_kernel, seq_len=seq_len),
        out_shape=[jax.ShapeDtypeStruct((n_seq * seq_len, n_heads * LANE), BF16),
                   jax.ShapeDtypeStruct(s0.shape, F32)],
        grid=(n_seq, n_heads),
        in_specs=[cspec("a_q"), cspec("a_ff"), cspec("a_fb"), cspec("a_i"), cspec("a_g"),
                  pl.BlockSpec((2, LANE), lambda b, h: (0, h)),
                  pl.BlockSpec((1, LANE), lambda b, h: (0, 0)),
                  st_spec] + [pl.BlockSpec(t.shape, lambda b, h: (0, 0)) for t in tables],
        out_specs=[pl.BlockSpec((seq_len, LANE), lambda b, h: (b, h)), st_spec],
        scratch_shapes=[pltpu.VMEM((seq_len, LANE), F32), pltpu.VMEM((seq_len, LANE), F32)],
        compiler_params=_params(("parallel", "parallel")),
        name="hgrn_%d" % seq_len,
    )(p_all, p_all, p_all, p_all, p_all, lb_l, norm_g.reshape(1, LANE), s0, *tables)


def _gmlp_kernel(u_ref, v_ref, gn_ref, ws_ref, bs_ref, o_ref, *, n_chunks, groups):
    u = _gelu_tanh(u_ref[...].astype(F32))
    v = (_rms(_gelu_tanh(v_ref[...].astype(F32))) * gn_ref[...]).astype(BF16)
    for n in range(n_chunks):
        rs = slice(n * MIX_CHUNK, (n + 1) * MIX_CHUNK)
        for g in range(groups):
            cs = slice(g * LANE, (g + 1) * LANE)
            mixed = jnp.dot(ws_ref[g], v[rs, cs], preferred_element_type=F32) + bs_ref[g]
            o_ref[rs, cs] = (u[rs, cs] * mixed).astype(BF16)


def _gmlp(p_all, col, norm_g, ws_bf, bias, tm):
    t = p_all.shape[0]
    groups = ws_bf.shape[0]
    cw = groups * LANE
    assert col["c_u"] % cw == 0 and col["c_v"] % cw == 0
    return pl.pallas_call(
        functools.partial(_gmlp_kernel, n_chunks=tm // MIX_CHUNK, groups=groups),
        out_shape=jax.ShapeDtypeStruct((t, cw), BF16),
        grid=(t // tm,),
        in_specs=[pl.BlockSpec((tm, cw), lambda i: (i, col["c_u"] // cw)),
                  pl.BlockSpec((tm, cw), lambda i: (i, col["c_v"] // cw)),
                  pl.BlockSpec((1, cw), lambda i: (0, 0)),
                  pl.BlockSpec((groups, MIX_CHUNK, MIX_CHUNK), lambda i: (0, 0, 0)),
                  pl.BlockSpec((groups, MIX_CHUNK, LANE), lambda i: (0, 0, 0))],
        out_specs=pl.BlockSpec((tm, cw), lambda i: (i, 0)),
        compiler_params=_params(("parallel",)),
        name="gmlp",
    )(p_all, p_all, norm_g.reshape(1, cw), ws_bf, bias)


def _route_rows(logit_rows):
    per = len(logit_rows) // N_GROUPS
    m = functools.reduce(jnp.maximum, logit_rows)
    p = [jnp.exp(r - m) for r in logit_rows]

    def top2_sum(a):
        pairs = [a[i] + a[j] for i in range(len(a)) for j in range(i + 1, len(a))]
        return functools.reduce(jnp.maximum, pairs)

    scores = [top2_sum(p[g * per:(g + 1) * per]) for g in range(N_GROUPS)]
    best, gi = scores[0], jnp.zeros(m.shape, jnp.int32)
    for g in range(1, N_GROUPS):
        better = scores[g] > best
        gi = jnp.where(better, g, gi)
        best = jnp.where(better, scores[g], best)
    a = []
    for j in range(per):
        aj = p[(N_GROUPS - 1) * per + j]
        for g in reversed(range(N_GROUPS - 1)):
            aj = jnp.where(gi == g, p[g * per + j], aj)
        a.append(aj)
    v1, i1 = a[0], jnp.zeros(m.shape, jnp.int32)
    for j in range(1, per):
        better = a[j] > v1
        i1 = jnp.where(better, j, i1)
        v1 = jnp.where(better, a[j], v1)
    v2, i2 = jnp.full(m.shape, -1.0, F32), jnp.zeros(m.shape, jnp.int32)
    for j in range(per):
        better = (i1 != j) & (a[j] > v2)
        i2 = jnp.where(better, j, i2)
        v2 = jnp.where(better, a[j], v2)
    inv = 1.0 / (v1 + v2)
    return gi * per + i1, gi * per + i2, v1 * inv, v2 * inv


def _merge_kernel(x_ref, ya_ref, yb_ref, yc_ref, ga0, ga1, gb0, gb1, gc0, gc1, wpa_ref, wpb_ref, wpc_ref,
                  wout_ref, g1_ref, n2_ref, sh2_ref, sc2_ref, rhi_ref, rlo_ref, rb_ref, x1_ref, h2_ref,
                  rt_ref, cnt_ref, *, n_experts):
    def gate(r0, r1):
        return jnp.concatenate([_sigmoid(r0[...].astype(F32)), _sigmoid(r1[...].astype(F32))], axis=-1)

    merged = gate(ga0, ga1) * jnp.dot(ya_ref[...], wpa_ref[...], preferred_element_type=F32)
    merged += gate(gb0, gb1) * jnp.dot(yb_ref[...], wpb_ref[...], preferred_element_type=F32)
    merged += gate(gc0, gc1) * jnp.dot(yc_ref[...], wpc_ref[...], preferred_element_type=F32)
    out = jnp.dot(merged.astype(BF16), wout_ref[...], preferred_element_type=F32)
    x1 = x_ref[...] + g1_ref[...] * out
    x1_ref[...] = x1
    h2 = _rms(x1) * n2_ref[...] * (1.0 + sc2_ref[...]) + sh2_ref[...]
    h2_ref[...] = h2.astype(BF16)
    hi, lo = _split2(h2)
    rhi, rlo = rhi_ref[...], rlo_ref[...]
    logits = (jnp.dot(hi, rhi, preferred_element_type=F32) + jnp.dot(hi, rlo, preferred_element_type=F32)
              + jnp.dot(lo, rhi, preferred_element_type=F32))
    lt = logits.T[0:n_experts, :] + rb_ref[...]
    tm = lt.shape[1]
    e1, e2, w1, w2 = _route_rows([lt[e:e + 1, :] for e in range(n_experts)])

    @pl.when(pl.program_id(0) == 0)
    def _():
        cnt_ref[...] = jnp.zeros_like(cnt_ref)

    e_id = lax.broadcasted_iota(jnp.int32, (n_experts, tm), 0)
    hit1, hit2 = e_id == e1, e_id == e2
    onehot = (hit1 | hit2).astype(BF16)
    src = lax.broadcasted_iota(jnp.int32, (tm, tm), 0)
    dst = lax.broadcasted_iota(jnp.int32, (tm, tm), 1)
    before = jnp.dot(onehot, (src < dst).astype(BF16), preferred_element_type=F32) + cnt_ref[...]
    rank1 = jnp.sum(jnp.where(hit1, before, 0.0), axis=0, keepdims=True)
    rank2 = jnp.sum(jnp.where(hit2, before, 0.0), axis=0, keepdims=True)
    cnt_ref[...] += jnp.dot(onehot, jnp.ones((tm, tm), BF16), preferred_element_type=F32)
    zero = jnp.zeros_like(w1)
    rt_ref[...] = jnp.concatenate([e1.astype(F32), e2.astype(F32), w1, w2, rank1, rank2, zero, zero], axis=0)


ROUTE_ROWS = 8


def _merge(rows, x, y_a, y_b, y_c, p_all, col, mod_l, norm2_g, w_pa, w_pb, w_pc, w_out, r_hi, r_lo, r_bias,
           tm):
    t, d = x.shape
    n_experts = r_bias.shape[0]
    gw = d // 2
    assert col["gates"] % gw == 0
    g0 = col["gates"] // gw
    row = lambda w: pl.BlockSpec((tm, w), lambda i: (i, 0))
    full = lambda a: pl.BlockSpec(a.shape, lambda i: (0,) * a.ndim, pipeline_mode=pl.Buffered(1))
    gate_specs = [pl.BlockSpec((tm, gw), functools.partial(lambda i, k: (i, g0 + k), k=k)) for k in range(6)]
    return pl.pallas_call(
        functools.partial(_merge_kernel, n_experts=n_experts),
        out_shape=[jax.ShapeDtypeStruct((t, d), F32), jax.ShapeDtypeStruct((t, d), BF16),
                   jax.ShapeDtypeStruct((ROUTE_ROWS, t), F32), jax.ShapeDtypeStruct((n_experts, tm), F32)],
        grid=(t // tm,),
        in_specs=[row(d), row(y_a.shape[1]), row(y_b.shape[1]), row(y_c.shape[1])] + gate_specs
                 + [full(w_pa), full(w_pb), full(w_pc), full(w_out), rows.mod_spec(2, tm, d),
                    pl.BlockSpec((1, d), lambda i: (0, 0)), rows.mod_spec(3, tm, d), rows.mod_spec(4, tm, d),
                    full(r_hi), full(r_lo), pl.BlockSpec((n_experts, tm), lambda i: (0, 0))],
        out_specs=[row(d), row(d), pl.BlockSpec((ROUTE_ROWS, tm), lambda i: (0, i)),
                   pl.BlockSpec((n_experts, tm), lambda i: (0, 0))],
        compiler_params=_params(("arbitrary",)),
        name="merge_out_proj",
    )(x, y_a, y_b, y_c, *([p_all] * 6), w_pa, w_pb, w_pc, w_out, mod_l, norm2_g.reshape(1, d), mod_l, mod_l,
      r_hi, r_lo, jnp.broadcast_to(r_bias.astype(F32)[:, None], (n_experts, tm)))


def _moe_kernel(be_ref, bv_ref, x_ref, wg_ref, wu_ref, wd_ref, o_ref):
    i = pl.program_id(0)

    @pl.when(bv_ref[i] > 0)
    def _():
        x = x_ref[...]
        a = jnp.dot(x, wg_ref[...], preferred_element_type=F32)
        b = jnp.dot(x, wu_ref[...], preferred_element_type=F32)
        h = (_silu(a) * b).astype(BF16)
        o_ref[...] = jnp.dot(h, wd_ref[...], preferred_element_type=F32).astype(BF16)

    @pl.when(bv_ref[i] == 0)
    def _():
        o_ref[...] = jnp.zeros_like(o_ref)


def _moe_ffn(xb, block_e, block_valid, wg, wu, wd, tm):
    n_slots, d = xb.shape
    ff = wg.shape[2]
    grid_spec = pltpu.PrefetchScalarGridSpec(
        num_scalar_prefetch=2,
        grid=(n_slots // tm,),
        in_specs=[pl.BlockSpec((tm, d), lambda i, be, bv: (i, 0)),
                  pl.BlockSpec((None, d, ff), lambda i, be, bv: (be[i], 0, 0)),
                  pl.BlockSpec((None, d, ff), lambda i, be, bv: (be[i], 0, 0)),
                  pl.BlockSpec((None, ff, d), lambda i, be, bv: (be[i], 0, 0))],
        out_specs=pl.BlockSpec((tm, d), lambda i, be, bv: (i, 0)),
    )
    return pl.pallas_call(
        _moe_kernel,
        out_shape=jax.ShapeDtypeStruct((n_slots, d), BF16),
        grid_spec=grid_spec,
        compiler_params=_params(("arbitrary",)),
        name="moe_experts",
    )(block_e, block_valid, xb, wg, wu, wd)


def _dispatch(expert, rank, counts, tm):
    n_experts = counts.shape[0]
    n_tok = expert.shape[1]
    n_assign = TOP_K * n_tok
    padded = (counts + tm - 1) // tm * tm
    pad_end = jnp.cumsum(padded)
    pad_start = pad_end - padded
    dest = jnp.take(pad_start, expert) + rank
    n_blocks = -(-(n_assign + n_experts * (tm - 1)) // tm)
    tok = jnp.broadcast_to(jnp.arange(n_tok, dtype=jnp.int32)[None, :], dest.shape)
    slot_tok = jnp.zeros((n_blocks * tm,), jnp.int32).at[dest.reshape(-1)].set(tok.reshape(-1))
    block_start = jnp.arange(n_blocks, dtype=jnp.int32) * tm
    block_e = jnp.minimum(jnp.sum(block_start[:, None] >= pad_end[None, :], axis=1), n_experts - 1)
    block_valid = (block_start < pad_end[-1]).astype(jnp.int32)
    return slot_tok, dest, block_e.astype(jnp.int32), block_valid


def _final_kernel(x_ref, moe_ref, g2_ref, nf_ref, o_ref):
    x = x_ref[...] + g2_ref[...] * moe_ref[...].astype(F32)
    o_ref[...] = _rms(x) * nf_ref[...]


def _final_norm(rows, x, moe, mod_l, norm_f, row0, n_rows, tm):
    d = x.shape[1]
    t0 = row0 // tm
    row_spec = pl.BlockSpec((tm, d), lambda i: (t0 + i, 0))
    g2_spec = pl.BlockSpec((None, 1, d), lambda i: (rows.cond_row(t0 + i, tm) * N_MOD + 5, 0, 0))
    return pl.pallas_call(
        _final_kernel,
        out_shape=jax.ShapeDtypeStruct((n_rows, d), F32),
        grid=(n_rows // tm,),
        in_specs=[row_spec, row_spec, g2_spec, pl.BlockSpec((1, d), lambda i: (0, 0))],
        out_specs=pl.BlockSpec((tm, d), lambda i: (i, 0)),
        compiler_params=_params(("parallel",)),
        name="final_norm",
    )(x, moe, mod_l, norm_f.reshape(1, d))


def kernel(x_prompt, x_sample, cache_k, cache_v, state_hgrn, c, c_ctx, w_mod, b_mod, norm1, norm2, norm_f,
           w_in, hgrn_lb, hgrn_norm, q_norm, k_norm, gmlp_norm, gmlp_ws, gmlp_bs, w_pa, w_pb, w_pc, w_out,
           router_w, router_b, exp_gate, exp_up, exp_down):
    n_ctx_seq, ctx_len, d = x_prompt.shape
    n_lat_seq, lat_len, _ = x_sample.shape
    depth = w_mod.shape[0]
    past, kvh = cache_k.shape[2], cache_k.shape[3]
    a_heads = state_hgrn.shape[3]
    a_width, b_width, c_width = hgrn_lb.shape[2], w_pb.shape[1], w_pc.shape[1]
    b_heads = b_width // LANE
    n_experts = router_w.shape[1]
    assert cache_k.shape[4] == LANE and state_hgrn.shape[4:] == (LANE, LANE) and a_width == a_heads * LANE

    sizes = [("a_q", a_width), ("a_ff", a_width), ("a_fb", a_width), ("a_i", a_width), ("a_g", a_width),
             ("b_q", b_width), ("b_k", kvh * LANE), ("b_v", kvh * LANE), ("c_u", c_width), ("c_v", c_width),
             ("gates", 3 * d)]
    col, acc = {}, 0
    for name, width in sizes:
        col[name] = acc
        acc += width
    assert acc == w_in.shape[2]

    n_ctx = n_ctx_seq * ctx_len
    rows = _Rows(n_ctx, n_lat_seq, lat_len)
    tm_proj = min(1024, n_ctx, lat_len)
    tn_proj = min(1024, math.gcd(w_in.shape[2], 1024))
    tm_row = min(256, n_ctx, lat_len)
    tm_moe = 256
    tq_lat = min(128, lat_len)

    cond = jnp.concatenate([c_ctx[None, :], c, jnp.zeros((COND_ROWS - 1 - n_lat_seq, d), F32)], axis=0)
    mod = _modulation(cond, w_mod, b_mod).reshape(depth, COND_ROWS * N_MOD, 1, d)

    p_lb = jax.nn.softmax(hgrn_lb.astype(F32), axis=1)
    lbs = jnp.cumsum(p_lb, axis=1) - p_lb[:, :1]

    cos, sin = _rope_tables(lat_len)
    cache_k2 = cache_k.reshape(n_lat_seq, depth, past, kvh * LANE)
    cache_v2 = cache_v.reshape(n_lat_seq, depth, past, kvh * LANE)
    s0_ctx = jnp.zeros((n_ctx_seq,) + state_hgrn.shape[2:], F32)
    r_pad = jnp.zeros((d, LANE), F32).at[:, :n_experts].set(router_w)
    r_hi, r_lo = _split2(r_pad)
    gmlp_bias = jnp.broadcast_to(gmlp_bs[:, :, :, None], gmlp_bs.shape + (LANE,))

    x = jnp.concatenate([x_prompt.reshape(n_ctx, d), x_sample.reshape(n_lat_seq * lat_len, d)], axis=0)
    moe = None
    new_k, new_v, new_s = [], [], []
    for l in range(depth):
        mod_l = mod[l]
        p_all, x = _in_proj(rows, x, moe, mod[l - 1] if l else None, mod_l, norm1[l], w_in[l].astype(BF16),
                            tm_proj, tn_proj)
        yb_ctx, k_l, v_l = _attention(p_all, col, 0, n_ctx_seq, ctx_len, b_heads, kvh, q_norm[l], k_norm[l],
                                      ctx_len)
        (yb_lat,) = _attention(p_all, col, n_ctx, n_lat_seq, lat_len, b_heads, kvh, q_norm[l], k_norm[l],
                               tq_lat, ctx=(cache_k2, cache_v2, l, cos, sin))
        ya_ctx, s_l = _hgrn(p_all, col, 0, n_ctx_seq, ctx_len, a_heads, lbs[:, l], hgrn_norm[l], s0_ctx)
        ya_lat, _ = _hgrn(p_all, col, n_ctx, n_lat_seq, lat_len, a_heads, lbs[:, l], hgrn_norm[l],
                          state_hgrn[:, l])
        y_c = _gmlp(p_all, col, gmlp_norm[l], gmlp_ws[l].astype(BF16), gmlp_bias[l], tm_proj)
        x, h2, route, counts = _merge(rows, x, jnp.concatenate([ya_ctx, ya_lat], axis=0),
                                      jnp.concatenate([yb_ctx, yb_lat], axis=0), y_c, p_all, col, mod_l,
                                      norm2[l], w_pa[l].astype(BF16), w_pb[l].astype(BF16),
                                      w_pc[l].astype(BF16), w_out[l].astype(BF16), r_hi, r_lo, router_b, tm_row)
        slot_tok, dest, block_e, block_valid = _dispatch(
            route[0:2].astype(jnp.int32), route[4:6].astype(jnp.int32), counts[:, 0].astype(jnp.int32), tm_moe)
        yb = _moe_ffn(jnp.take(h2, slot_tok, axis=0), block_e, block_valid, exp_gate[l].astype(BF16),
                      exp_up[l].astype(BF16), exp_down[l].astype(BF16), tm_moe)
        moe = (jnp.take(yb, dest[0], axis=0).astype(F32) * route[2][:, None]
               + jnp.take(yb, dest[1], axis=0).astype(F32) * route[3][:, None]).astype(BF16)
        new_k.append(k_l)
        new_v.append(v_l)
        new_s.append(s_l)

    y_ctx = _final_norm(rows, x, moe, mod[depth - 1], norm_f, 0, n_ctx, tm_row)
    y_lat = _final_norm(rows, x, moe, mod[depth - 1], norm_f, n_ctx, n_lat_seq * lat_len, tm_row)
    kv_shape = (n_ctx_seq, depth, ctx_len, kvh, LANE)
    return (y_ctx.reshape(n_ctx_seq, ctx_len, d), y_lat.reshape(n_lat_seq, lat_len, d),
            jnp.stack(new_k, axis=1).reshape(kv_shape), jnp.stack(new_v, axis=1).reshape(kv_shape),
            jnp.stack(new_s, axis=1))
```

```python
import functools
import math

import jax
import jax.numpy as jnp
import numpy as np
from jax import lax
from jax.experimental import pallas as pl
from jax.experimental.pallas import tpu as pltpu

F32 = jnp.float32
BF16 = jnp.bfloat16

EPS = 1e-6
N_MOD = 6
GRID_W = 64
ROPE_THETA = 10000.0
MIX_CHUNK = 128
N_GROUPS = 4
TOP_K = 2
LANE = 128
SUBLANES = 8
HGRN_BLOCK = 128
HGRN_LEVELS = 7
ATTN_KEY_CHUNK = 1024
LOG2E = 1.4426950408889634
COND_ROWS = 16
VMEM_LIMIT = 56 * 1024 * 1024

NT_DIMS = (((1,), (1,)), ((), ()))
TN_DIMS = (((0,), (0,)), ((), ()))


def _sigmoid(x):
    return 1.0 / (1.0 + jnp.exp(-x))


def _silu(x):
    return x * _sigmoid(x)


def _gelu_tanh(x):
    return 0.5 * x * (1.0 + jnp.tanh(math.sqrt(2.0 / math.pi) * (x + 0.044715 * (x * x * x))))


def _rms(x):
    return x * lax.rsqrt(jnp.mean(x * x, axis=-1, keepdims=True) + EPS)


def _split2(x):
    hi = x.astype(BF16)
    lo = (x - hi.astype(F32)).astype(BF16)
    return hi, lo


def _split3(x):
    hi = x.astype(BF16)
    r = x - hi.astype(F32)
    mid = r.astype(BF16)
    lo = (r - mid.astype(F32)).astype(BF16)
    return hi, mid, lo


def _params(sem):
    return pltpu.CompilerParams(dimension_semantics=sem, vmem_limit_bytes=VMEM_LIMIT)


def _mod_kernel(c_ref, w_ref, b_ref, o_ref):
    s = _silu(c_ref[...]).astype(BF16)
    o_ref[...] = jnp.dot(s, w_ref[...].astype(BF16), preferred_element_type=F32) + b_ref[...]


def _modulation(cond, w_mod, b_mod):
    depth, d, n = w_mod.shape
    tn = math.gcd(n, 1024)
    return pl.pallas_call(
        _mod_kernel,
        out_shape=jax.ShapeDtypeStruct((depth, COND_ROWS, n), F32),
        grid=(depth, n // tn),
        in_specs=[
            pl.BlockSpec((COND_ROWS, d), lambda l, j: (0, 0)),
            pl.BlockSpec((None, d, tn), lambda l, j: (l, 0, j)),
            pl.BlockSpec((None, 1, tn), lambda l, j: (l, 0, j)),
        ],
        out_specs=pl.BlockSpec((None, COND_ROWS, tn), lambda l, j: (l, 0, j)),
        compiler_params=_params(("parallel", "parallel")),
        name="adaln_modulation",
    )(cond, w_mod, b_mod.reshape(depth, 1, n))


class _Rows:
    def __init__(self, n_ctx, n_lat_seq, lat_len):
        self.n_ctx, self.lat_len = n_ctx, lat_len
        self.total = n_ctx + n_lat_seq * lat_len

    def cond_row(self, i, tm):
        ctx_tiles = self.n_ctx // tm
        per_seq = self.lat_len // tm
        return jnp.where(i < ctx_tiles, 0, 1 + (i - ctx_tiles) // per_seq)

    def mod_spec(self, k, tm, d):
        return pl.BlockSpec((None, 1, d), lambda i, *_: (self.cond_row(i, tm) * N_MOD + k, 0, 0))


def _in_proj_kernel(*refs, combine):
    if combine:
        x_ref, moe_ref, g2_ref, n_ref, sh_ref, sc_ref, w_ref, o_ref, x2_ref, h_scr = refs
    else:
        x_ref, n_ref, sh_ref, sc_ref, w_ref, o_ref, h_scr = refs

    @pl.when(pl.program_id(1) == 0)
    def _():
        x = x_ref[...]
        if combine:
            x = x + g2_ref[...] * moe_ref[...].astype(F32)
            x2_ref[...] = x
        h = _rms(x) * n_ref[...] * (1.0 + sc_ref[...]) + sh_ref[...]
        h_scr[...] = h.astype(BF16)

    o_ref[...] = jnp.dot(h_scr[...], w_ref[...], preferred_element_type=F32).astype(BF16)


def _in_proj(rows, x, moe, mod_prev, mod_l, norm_g, w_bf, tm, tn):
    t, d = x.shape
    n = w_bf.shape[1]
    combine = moe is not None
    row_spec = pl.BlockSpec((tm, d), lambda i, j: (i, 0))
    row_in_spec = pl.BlockSpec((tm, d), lambda i, j: (i, 0), pipeline_mode=pl.Buffered(1))
    vec_spec = pl.BlockSpec((1, d), lambda i, j: (0, 0))
    in_specs = [row_in_spec]
    args = [x]
    if combine:
        in_specs += [row_in_spec, rows.mod_spec(5, tm, d)]
        args += [moe, mod_prev]
    in_specs += [vec_spec, rows.mod_spec(0, tm, d), rows.mod_spec(1, tm, d),
                 pl.BlockSpec((d, tn), lambda i, j: (0, j))]
    args += [norm_g.reshape(1, d), mod_l, mod_l, w_bf]
    out_shape = [jax.ShapeDtypeStruct((t, n), BF16)]
    out_specs = [pl.BlockSpec((tm, tn), lambda i, j: (i, j))]
    if combine:
        out_shape.append(jax.ShapeDtypeStruct((t, d), F32))
        out_specs.append(row_spec)
    res = pl.pallas_call(
        functools.partial(_in_proj_kernel, combine=combine),
        out_shape=out_shape,
        grid=(t // tm, n // tn),
        in_specs=in_specs,
        out_specs=out_specs,
        scratch_shapes=[pltpu.VMEM((tm, d), BF16)],
        compiler_params=_params(("parallel", "arbitrary")),
        name="in_proj",
    )(*args)
    return (res[0], res[1]) if combine else (res[0], x)


def _rope(x, cos, sin_signed):
    lane = lax.broadcasted_iota(jnp.int32, x.shape, 1)
    first = (lane // (LANE // 4)) % 2 == 0
    swapped = jnp.where(first, pltpu.roll(x, LANE - LANE // 4, 1), pltpu.roll(x, LANE // 4, 1))
    return x * cos + swapped * sin_signed


def _attn_kernel(*refs, latent, past, tq, grp, scale):
    if latent:
        (q_ref, kn_ref, vn_ref, ck_ref, cv_ref, cos_ref, sin_ref, qg_ref, kg_ref,
         o_ref, k_scr, v_scr) = refs
    else:
        q_ref, kn_ref, vn_ref, qg_ref, kg_ref, o_ref, ko_ref, vo_ref, k_scr, v_scr = refs
    qi = pl.program_id(2)

    @pl.when(qi == 0)
    def _():
        k = _rms(kn_ref[...].astype(F32)) * kg_ref[...]
        if latent:
            k = _rope(k, cos_ref[...], sin_ref[...])
            k_scr[0:past, :] = ck_ref[...].astype(BF16)
            v_scr[0:past, 0:LANE] = cv_ref[...].astype(BF16)
        else:
            ko_ref[...] = k
            vo_ref[...] = vn_ref[...].astype(F32)
        k_scr[past:, :] = k.astype(BF16)
        v_scr[past:, 0:LANE] = vn_ref[...]
        v_scr[:, LANE:] = jnp.ones((v_scr.shape[0], LANE), BF16)

    q = q_ref[...].astype(F32)
    heads = []
    for h in range(grp):
        qh = _rms(q[:, h * LANE:(h + 1) * LANE]) * qg_ref[...]
        if latent:
            r0 = pl.multiple_of(qi * tq, tq)
            qh = _rope(qh, cos_ref[pl.ds(r0, tq), :], sin_ref[pl.ds(r0, tq), :])
        heads.append((qh * scale).astype(BF16))
    qs = jnp.concatenate(heads, axis=0)
    n_keys = k_scr.shape[0]
    m = acc = None
    for c0 in range(0, n_keys, ATTN_KEY_CHUNK):
        c1 = min(c0 + ATTN_KEY_CHUNK, n_keys)
        s = lax.dot_general(qs, k_scr[c0:c1, :], NT_DIMS, preferred_element_type=F32)
        m_c = jnp.max(functools.reduce(jnp.maximum, [s[:, t:t + LANE] for t in range(0, c1 - c0, LANE)]),
                      axis=-1, keepdims=True)
        m_new = m_c if m is None else jnp.maximum(m, m_c)
        pv = jnp.dot(jnp.exp2(s - m_new).astype(BF16), v_scr[c0:c1, :], preferred_element_type=F32)
        acc = pv if m is None else acc * jnp.exp2(m - m_new) + pv
        m = m_new
    o = acc[:, 0:LANE] / acc[:, LANE:]
    for h in range(grp):
        o_ref[:, h * LANE:(h + 1) * LANE] = o[h * tq:(h + 1) * tq, :].astype(BF16)


def _attention(p_all, col, row0, n_seq, seq_len, n_heads, kvh, q_gain, k_gain, tq, ctx=None):
    grp = n_heads // kvh
    latent = ctx is not None
    qw = grp * LANE
    assert col["b_q"] % qw == 0 and row0 % seq_len == 0 and seq_len % tq == 0
    seq0, nq = row0 // seq_len, seq_len // tq
    q_spec = pl.BlockSpec((tq, qw), lambda b, g, i: (seq0 * nq + b * nq + i, col["b_q"] // qw + g))
    k_spec = pl.BlockSpec((seq_len, LANE), lambda b, g, i: (seq0 + b, col["b_k"] // LANE + g))
    v_spec = pl.BlockSpec((seq_len, LANE), lambda b, g, i: (seq0 + b, col["b_v"] // LANE + g))
    gain_spec = pl.BlockSpec((1, LANE), lambda b, g, i: (0, 0))
    in_specs = [q_spec, k_spec, v_spec]
    args = [p_all, p_all, p_all]
    past = 0
    if latent:
        cache_k, cache_v, layer, cos, sin = ctx
        past = cache_k.shape[2]
        c_spec = pl.BlockSpec((None, None, past, LANE), lambda b, g, i: (b, layer, 0, g))
        t_spec = pl.BlockSpec((seq_len, LANE), lambda b, g, i: (0, 0))
        in_specs += [c_spec, c_spec, t_spec, t_spec]
        args += [cache_k, cache_v, cos, sin]
    in_specs += [gain_spec, gain_spec]
    args += [q_gain.reshape(1, LANE), k_gain.reshape(1, LANE)]
    out_shape = [jax.ShapeDtypeStruct((n_seq * seq_len, n_heads * LANE), BF16)]
    out_specs = [pl.BlockSpec((tq, qw), lambda b, g, i: (b * nq + i, g))]
    if not latent:
        kv_shape = jax.ShapeDtypeStruct((n_seq, seq_len, kvh * LANE), F32)
        kv_spec = pl.BlockSpec((None, seq_len, LANE), lambda b, g, i: (b, 0, g))
        out_shape += [kv_shape, kv_shape]
        out_specs += [kv_spec, kv_spec]
    return pl.pallas_call(
        functools.partial(_attn_kernel, latent=latent, past=past, tq=tq, grp=grp, scale=LANE ** -0.5 * LOG2E),
        out_shape=out_shape,
        grid=(n_seq, kvh, nq),
        in_specs=in_specs,
        out_specs=out_specs,
        scratch_shapes=[pltpu.VMEM((past + seq_len, LANE), BF16), pltpu.VMEM((past + seq_len, 2 * LANE), BF16)],
        compiler_params=_params(("parallel", "parallel", "arbitrary")),
        name="attention_latent" if latent else "attention_context",
    )(*args)


def _rope_tables(seq_len):
    quarter = LANE // 4
    inv = ROPE_THETA ** (-jnp.arange(quarter, dtype=F32) / quarter)
    pos = jnp.arange(seq_len)
    row = (pos // GRID_W).astype(F32)[:, None] * inv[None, :]
    colp = (pos % GRID_W).astype(F32)[:, None] * inv[None, :]
    cos = jnp.concatenate([jnp.cos(row), jnp.cos(row), jnp.cos(colp), jnp.cos(colp)], axis=-1)
    sin = jnp.concatenate([-jnp.sin(row), jnp.sin(row), -jnp.sin(colp), jnp.sin(colp)], axis=-1)
    return cos, sin


def _hgrn_level_codes():
    r = np.arange(HGRN_BLOCK)[:, None]
    u = np.arange(HGRN_BLOCK)[None, :]
    diff = r ^ u
    code = np.where(diff == 0, 0, 1 + np.floor(np.log2(np.maximum(diff, 1))).astype(np.int32))
    return [jnp.asarray(np.where(keep, code, -1), jnp.int32) for keep in (r >= u, r <= u)]


def _segment_runs(x):
    n_tiles = x.shape[0] // SUBLANES
    sub = lax.broadcasted_iota(jnp.int32, (SUBLANES, x.shape[1]), 0)
    pre_t, suf_t = [], []
    for i in range(n_tiles):
        p = s = x[i * SUBLANES:(i + 1) * SUBLANES]
        pre_i, suf_i = [], []
        n = 1
        while n < SUBLANES:
            pos = sub % (2 * n)
            add_p, add_s = jnp.zeros_like(p), jnp.zeros_like(s)
            for j in range(n):
                add_p = jnp.where(pos == n + j, pltpu.roll(p, j + 1, 0), add_p)
                add_s = jnp.where(pos == n - 1 - j, pltpu.roll(s, SUBLANES - (j + 1), 0), add_s)
            p, s = p + add_p, s + add_s
            pre_i.append(p)
            suf_i.append(s)
            n *= 2
        pre_t.append(pre_i)
        suf_t.append(suf_i)
    levels = len(pre_t[0])
    pre = [[t[m] for t in pre_t] for m in range(levels)]
    suf = [[t[m] for t in suf_t] for m in range(levels)]
    cur_p, cur_s = pre[-1], suf[-1]
    tot = [jnp.broadcast_to(t[SUBLANES - 1:SUBLANES, :], t.shape) for t in cur_p]
    seg = 1
    while seg < n_tiles:
        new_p, new_s, new_tot = list(cur_p), list(cur_s), list(tot)
        for base in range(0, n_tiles, 2 * seg):
            t_first, t_second = tot[base], tot[base + seg]
            both = t_first + t_second
            for i in range(base, base + seg):
                new_s[i] = cur_s[i] + t_second
                new_p[i + seg] = cur_p[i + seg] + t_first
                new_tot[i] = new_tot[i + seg] = both
        cur_p, cur_s, tot = new_p, new_s, new_tot
        pre.append(cur_p)
        suf.append(cur_s)
        seg *= 2
    join = lambda tiles: jnp.concatenate(tiles, axis=0)
    return [join(t) for t in pre], [join(t) for t in suf]


def _hgrn_kernel(q_ref, ff_ref, fb_ref, i_ref, g_ref, lb_ref, ng_ref, s0_ref, lvlf_ref, lvlb_ref, y_ref,
                 sfin_ref, of_scr, ob_scr, *, seq_len):
    blk, n_lvl = HGRN_BLOCK, HGRN_LEVELS
    n_blk = seq_len // blk

    def scan_blocks(chains):
        q, k, v, q_run, k_run = [], [], [], [], []
        for r0, f_ref, lb, _, forward, _, _ in chains:
            q.append(_silu(q_ref[pl.ds(r0, blk), :].astype(F32)))
            g = lb + (1.0 - lb) * _sigmoid(f_ref[pl.ds(r0, blk), :].astype(F32))
            k.append(1.0 - g)
            v.append(i_ref[pl.ds(r0, blk), :])
            log_g = jnp.log(g) * LOG2E
            pre, suf = _segment_runs(log_g)
            near, far = (pre, suf) if forward else (suf, pre)
            q_run.append([log_g] + near)
            k_run.append([None] + [x - log_g for x in far])
        kb = [x.astype(BF16) for x in k]
        level = [c[5][...] for c in chains]
        score = [jnp.where(lv == 0, lax.dot_general(qc.astype(BF16), kc, NT_DIMS, preferred_element_type=F32), 0.0)
                 for lv, qc, kc in zip(level, q, kb)]
        for m in range(n_lvl):
            for i in range(len(chains)):
                qs = (q[i] * jnp.exp2(q_run[i][m])).astype(BF16)
                ks = kb[i] if m == 0 else (k[i] * jnp.exp2(k_run[i][m])).astype(BF16)
                score[i] = jnp.where(level[i] == m + 1,
                                     lax.dot_general(qs, ks, NT_DIMS, preferred_element_type=F32), score[i])
        out = []
        for i, (_, _, _, st, _, _, edge) in enumerate(chains):
            q_all = jnp.exp2(q_run[i][n_lvl])
            o = jnp.dot(score[i].astype(BF16), v[i], preferred_element_type=F32)
            o += lax.dot_general((q[i] * q_all).astype(BF16), st.astype(BF16), NT_DIMS,
                                 preferred_element_type=F32)
            kd = (k[i] * jnp.exp2(k_run[i][n_lvl])).astype(BF16)
            st = st * q_all[edge:edge + 1, :] + lax.dot_general(v[i], kd, TN_DIMS, preferred_element_type=F32)
            out.append((o, st))
        return out

    lb_f, lb_b = lb_ref[0:1, :], lb_ref[1:2, :]

    def body(j, carry):
        st_f, st_b = carry
        r_f = pl.multiple_of(j * blk, blk)
        r_b = pl.multiple_of((n_blk - 1 - j) * blk, blk)
        (o_f, st_f), (o_b, st_b) = scan_blocks([(r_f, ff_ref, lb_f, st_f, True, lvlf_ref, blk - 1),
                                                (r_b, fb_ref, lb_b, st_b, False, lvlb_ref, 0)])
        of_scr[pl.ds(r_f, blk), :] = o_f
        ob_scr[pl.ds(r_b, blk), :] = o_b
        return st_f, st_b

    st_f, st_b = lax.fori_loop(0, n_blk, body, (s0_ref[0].T, s0_ref[1].T))
    sfin_ref[0] = st_f.T
    sfin_ref[1] = st_b.T
    o = _rms(of_scr[...] + ob_scr[...]) * ng_ref[...]
    y_ref[...] = (o * _silu(g_ref[...].astype(F32))).astype(BF16)


def _hgrn(p_all, col, row0, n_seq, seq_len, n_heads, lb_l, norm_g, s0):
    assert row0 % seq_len == 0 and seq_len % HGRN_BLOCK == 0
    seq0 = row0 // seq_len

    def cspec(name):
        return pl.BlockSpec((seq_len, LANE), lambda b, h: (seq0 + b, col[name] // LANE + h))

    st_spec = pl.BlockSpec((None, 2, None, LANE, LANE), lambda b, h: (b, 0, h, 0, 0))
    tables = _hgrn_level_codes()
    return pl.pallas_call(
        functools.partial(_hgrn_kernel, seq_len=seq_len),
        out_shape=[jax.ShapeDtypeStruct((n_seq * seq_len, n_heads * LANE), BF16),
                   jax.ShapeDtypeStruct(s0.shape, F32)],
        grid=(n_seq, n_heads),
        in_specs=[cspec("a_q"), cspec("a_ff"), cspec("a_fb"), cspec("a_i"), cspec("a_g"),
                  pl.BlockSpec((2, LANE), lambda b, h: (0, h)),
                  pl.BlockSpec((1, LANE), lambda b, h: (0, 0)),
                  st_spec] + [pl.BlockSpec(t.shape, lambda b, h: (0, 0)) for t in tables],
        out_specs=[pl.BlockSpec((seq_len, LANE), lambda b, h: (b, h)), st_spec],
        scratch_shapes=[pltpu.VMEM((seq_len, LANE), F32), pltpu.VMEM((seq_len, LANE), F32)],
        compiler_params=_params(("parallel", "parallel")),
        name="hgrn_%d" % seq_len,
    )(p_all, p_all, p_all, p_all, p_all, lb_l, norm_g.reshape(1, LANE), s0, *tables)


def _gmlp_kernel(u_ref, v_ref, gn_ref, ws_ref, bs_ref, o_ref, *, n_chunks, groups):
    u = _gelu_tanh(u_ref[...].astype(F32))
    v = (_rms(_gelu_tanh(v_ref[...].astype(F32))) * gn_ref[...]).astype(BF16)
    for n in range(n_chunks):
        rs = slice(n * MIX_CHUNK, (n + 1) * MIX_CHUNK)
        for g in range(groups):
            cs = slice(g * LANE, (g + 1) * LANE)
            mixed = jnp.dot(ws_ref[g], v[rs, cs], preferred_element_type=F32) + bs_ref[g]
            o_ref[rs, cs] = (u[rs, cs] * mixed).astype(BF16)


def _gmlp(p_all, col, norm_g, ws_bf, bias, tm):
    t = p_all.shape[0]
    groups = ws_bf.shape[0]
    cw = groups * LANE
    assert col["c_u"] % cw == 0 and col["c_v"] % cw == 0
    return pl.pallas_call(
        functools.partial(_gmlp_kernel, n_chunks=tm // MIX_CHUNK, groups=groups),
        out_shape=jax.ShapeDtypeStruct((t, cw), BF16),
        grid=(t // tm,),
        in_specs=[pl.BlockSpec((tm, cw), lambda i: (i, col["c_u"] // cw)),
                  pl.BlockSpec((tm, cw), lambda i: (i, col["c_v"] // cw)),
                  pl.BlockSpec((1, cw), lambda i: (0, 0)),
                  pl.BlockSpec((groups, MIX_CHUNK, MIX_CHUNK), lambda i: (0, 0, 0)),
                  pl.BlockSpec((groups, MIX_CHUNK, LANE), lambda i: (0, 0, 0))],
        out_specs=pl.BlockSpec((tm, cw), lambda i: (i, 0)),
        compiler_params=_params(("parallel",)),
        name="gmlp",
    )(p_all, p_all, norm_g.reshape(1, cw), ws_bf, bias)


def _route_rows(logit_rows):
    per = len(logit_rows) // N_GROUPS
    m = functools.reduce(jnp.maximum, logit_rows)
    p = [jnp.exp(r - m) for r in logit_rows]

    def top2_sum(a):
        pairs = [a[i] + a[j] for i in range(len(a)) for j in range(i + 1, len(a))]
        return functools.reduce(jnp.maximum, pairs)

    scores = [top2_sum(p[g * per:(g + 1) * per]) for g in range(N_GROUPS)]
    best, gi = scores[0], jnp.zeros(m.shape, jnp.int32)
    for g in range(1, N_GROUPS):
        better = scores[g] > best
        gi = jnp.where(better, g, gi)
        best = jnp.where(better, scores[g], best)
    a = []
    for j in range(per):
        aj = p[(N_GROUPS - 1) * per + j]
        for g in reversed(range(N_GROUPS - 1)):
            aj = jnp.where(gi == g, p[g * per + j], aj)
        a.append(aj)
    v1, i1 = a[0], jnp.zeros(m.shape, jnp.int32)
    for j in range(1, per):
        better = a[j] > v1
        i1 = jnp.where(better, j, i1)
        v1 = jnp.where(better, a[j], v1)
    v2, i2 = jnp.full(m.shape, -1.0, F32), jnp.zeros(m.shape, jnp.int32)
    for j in range(per):
        better = (i1 != j) & (a[j] > v2)
        i2 = jnp.where(better, j, i2)
        v2 = jnp.where(better, a[j], v2)
    inv = 1.0 / (v1 + v2)
    return gi * per + i1, gi * per + i2, v1 * inv, v2 * inv


def _merge_kernel(x_ref, ya_ref, yb_ref, yc_ref, ga0, ga1, gb0, gb1, gc0, gc1, wpa_ref, wpb_ref, wpc_ref,
                  wout_ref, g1_ref, n2_ref, sh2_ref, sc2_ref, rhi_ref, rlo_ref, rb_ref, x1_ref, h2_ref,
                  rt_ref, cnt_ref, *, n_experts):
    def gate(r0, r1):
        return jnp.concatenate([_sigmoid(r0[...].astype(F32)), _sigmoid(r1[...].astype(F32))], axis=-1)

    merged = gate(ga0, ga1) * jnp.dot(ya_ref[...], wpa_ref[...], preferred_element_type=F32)
    merged += gate(gb0, gb1) * jnp.dot(yb_ref[...], wpb_ref[...], preferred_element_type=F32)
    merged += gate(gc0, gc1) * jnp.dot(yc_ref[...], wpc_ref[...], preferred_element_type=F32)
    out = jnp.dot(merged.astype(BF16), wout_ref[...], preferred_element_type=F32)
    x1 = x_ref[...] + g1_ref[...] * out
    x1_ref[...] = x1
    h2 = _rms(x1) * n2_ref[...] * (1.0 + sc2_ref[...]) + sh2_ref[...]
    h2_ref[...] = h2.astype(BF16)
    hi, lo = _split2(h2)
    rhi, rlo = rhi_ref[...], rlo_ref[...]
    logits = (jnp.dot(hi, rhi, preferred_element_type=F32) + jnp.dot(hi, rlo, preferred_element_type=F32)
              + jnp.dot(lo, rhi, preferred_element_type=F32))
    lt = logits.T[0:n_experts, :] + rb_ref[...]
    tm = lt.shape[1]
    e1, e2, w1, w2 = _route_rows([lt[e:e + 1, :] for e in range(n_experts)])

    @pl.when(pl.program_id(0) == 0)
    def _():
        cnt_ref[...] = jnp.zeros_like(cnt_ref)

    e_id = lax.broadcasted_iota(jnp.int32, (n_experts, tm), 0)
    hit1, hit2 = e_id == e1, e_id == e2
    onehot = (hit1 | hit2).astype(BF16)
    src = lax.broadcasted_iota(jnp.int32, (tm, tm), 0)
    dst = lax.broadcasted_iota(jnp.int32, (tm, tm), 1)
    before = jnp.dot(onehot, (src < dst).astype(BF16), preferred_element_type=F32) + cnt_ref[...]
    rank1 = jnp.sum(jnp.where(hit1, before, 0.0), axis=0, keepdims=True)
    rank2 = jnp.sum(jnp.where(hit2, before, 0.0), axis=0, keepdims=True)
    cnt_ref[...] += jnp.dot(onehot, jnp.ones((tm, tm), BF16), preferred_element_type=F32)
    zero = jnp.zeros_like(w1)
    rt_ref[...] = jnp.concatenate([e1.astype(F32), e2.astype(F32), w1, w2, rank1, rank2, zero, zero], axis=0)


ROUTE_ROWS = 8


def _merge(rows, x, y_a, y_b, y_c, p_all, col, mod_l, norm2_g, w_pa, w_pb, w_pc, w_out, r_hi, r_lo, r_bias,
           tm):
    t, d = x.shape
    n_experts = r_bias.shape[0]
    gw = d // 2
    assert col["gates"] % gw == 0
    g0 = col["gates"] // gw
    row = lambda w: pl.BlockSpec((tm, w), lambda i: (i, 0))
    full = lambda a: pl.BlockSpec(a.shape, lambda i: (0,) * a.ndim, pipeline_mode=pl.Buffered(1))
    gate_specs = [pl.BlockSpec((tm, gw), functools.partial(lambda i, k: (i, g0 + k), k=k)) for k in range(6)]
    return pl.pallas_call(
        functools.partial(_merge_kernel, n_experts=n_experts),
        out_shape=[jax.ShapeDtypeStruct((t, d), F32), jax.ShapeDtypeStruct((t, d), BF16),
                   jax.ShapeDtypeStruct((ROUTE_ROWS, t), F32), jax.ShapeDtypeStruct((n_experts, tm), F32)],
        grid=(t // tm,),
        in_specs=[row(d), row(y_a.shape[1]), row(y_b.shape[1]), row(y_c.shape[1])] + gate_specs
                 + [full(w_pa), full(w_pb), full(w_pc), full(w_out), rows.mod_spec(2, tm, d),
                    pl.BlockSpec((1, d), lambda i: (0, 0)), rows.mod_spec(3, tm, d), rows.mod_spec(4, tm, d),
                    full(r_hi), full(r_lo), pl.BlockSpec((n_experts, tm), lambda i: (0, 0))],
        out_specs=[row(d), row(d), pl.BlockSpec((ROUTE_ROWS, tm), lambda i: (0, i)),
                   pl.BlockSpec((n_experts, tm), lambda i: (0, 0))],
        compiler_params=_params(("arbitrary",)),
        name="merge_out_proj",
    )(x, y_a, y_b, y_c, *([p_all] * 6), w_pa, w_pb, w_pc, w_out, mod_l, norm2_g.reshape(1, d), mod_l, mod_l,
      r_hi, r_lo, jnp.broadcast_to(r_bias.astype(F32)[:, None], (n_experts, tm)))


def _moe_kernel(be_ref, bv_ref, x_ref, wg_ref, wu_ref, wd_ref, o_ref):
    i = pl.program_id(0)

    @pl.when(bv_ref[i] > 0)
    def _():
        x = x_ref[...]
        a = jnp.dot(x, wg_ref[...], preferred_element_type=F32)
        b = jnp.dot(x, wu_ref[...], preferred_element_type=F32)
        h = (_silu(a) * b).astype(BF16)
        o_ref[...] = jnp.dot(h, wd_ref[...], preferred_element_type=F32).astype(BF16)

    @pl.when(bv_ref[i] == 0)
    def _():
        o_ref[...] = jnp.zeros_like(o_ref)


def _moe_ffn(xb, block_e, block_valid, wg, wu, wd, tm):
    n_slots, d = xb.shape
    ff = wg.shape[2]
    grid_spec = pltpu.PrefetchScalarGridSpec(
        num_scalar_prefetch=2,
        grid=(n_slots // tm,),
        in_specs=[pl.BlockSpec((tm, d), lambda i, be, bv: (i, 0)),
                  pl.BlockSpec((None, d, ff), lambda i, be, bv: (be[i], 0, 0)),
                  pl.BlockSpec((None, d, ff), lambda i, be, bv: (be[i], 0, 0)),
                  pl.BlockSpec((None, ff, d), lambda i, be, bv: (be[i], 0, 0))],
        out_specs=pl.BlockSpec((tm, d), lambda i, be, bv: (i, 0)),
    )
    return pl.pallas_call(
        _moe_kernel,
        out_shape=jax.ShapeDtypeStruct((n_slots, d), BF16),
        grid_spec=grid_spec,
        compiler_params=_params(("arbitrary",)),
        name="moe_experts",
    )(block_e, block_valid, xb, wg, wu, wd)


def _dispatch(expert, rank, counts, tm):
    n_experts = counts.shape[0]
    n_tok = expert.shape[1]
    n_assign = TOP_K * n_tok
    padded = (counts + tm - 1) // tm * tm
    pad_end = jnp.cumsum(padded)
    pad_start = pad_end - padded
    first_slot = jnp.sum(jnp.where(expert[..., None] == jnp.arange(n_experts), pad_start, 0), axis=-1)
    dest = first_slot + rank
    n_blocks = -(-(n_assign + n_experts * (tm - 1)) // tm)
    tok = jnp.broadcast_to(jnp.arange(n_tok, dtype=jnp.int32)[None, :], dest.shape)
    slot_tok = jnp.zeros((n_blocks * tm,), jnp.int32).at[dest.reshape(-1)].set(tok.reshape(-1))
    block_start = jnp.arange(n_blocks, dtype=jnp.int32) * tm
    block_e = jnp.minimum(jnp.sum(block_start[:, None] >= pad_end[None, :], axis=1), n_experts - 1)
    block_valid = (block_start < pad_end[-1]).astype(jnp.int32)
    return slot_tok, dest, block_e.astype(jnp.int32), block_valid


def _final_kernel(x_ref, moe_ref, g2_ref, nf_ref, o_ref):
    x = x_ref[...] + g2_ref[...] * moe_ref[...].astype(F32)
    o_ref[...] = _rms(x) * nf_ref[...]


def _final_norm(rows, x, moe, mod_l, norm_f, row0, n_rows, tm):
    d = x.shape[1]
    t0 = row0 // tm
    row_spec = pl.BlockSpec((tm, d), lambda i: (t0 + i, 0))
    g2_spec = pl.BlockSpec((None, 1, d), lambda i: (rows.cond_row(t0 + i, tm) * N_MOD + 5, 0, 0))
    return pl.pallas_call(
        _final_kernel,
        out_shape=jax.ShapeDtypeStruct((n_rows, d), F32),
        grid=(n_rows // tm,),
        in_specs=[row_spec, row_spec, g2_spec, pl.BlockSpec((1, d), lambda i: (0, 0))],
        out_specs=pl.BlockSpec((tm, d), lambda i: (i, 0)),
        compiler_params=_params(("parallel",)),
        name="final_norm",
    )(x, moe, mod_l, norm_f.reshape(1, d))


def kernel(x_prompt, x_sample, cache_k, cache_v, state_hgrn, c, c_ctx, w_mod, b_mod, norm1, norm2, norm_f,
           w_in, hgrn_lb, hgrn_norm, q_norm, k_norm, gmlp_norm, gmlp_ws, gmlp_bs, w_pa, w_pb, w_pc, w_out,
           router_w, router_b, exp_gate, exp_up, exp_down):
    n_ctx_seq, ctx_len, d = x_prompt.shape
    n_lat_seq, lat_len, _ = x_sample.shape
    depth = w_mod.shape[0]
    past, kvh = cache_k.shape[2], cache_k.shape[3]
    a_heads = state_hgrn.shape[3]
    a_width, b_width, c_width = hgrn_lb.shape[2], w_pb.shape[1], w_pc.shape[1]
    b_heads = b_width // LANE
    n_experts = router_w.shape[1]
    assert cache_k.shape[4] == LANE and state_hgrn.shape[4:] == (LANE, LANE) and a_width == a_heads * LANE

    sizes = [("a_q", a_width), ("a_ff", a_width), ("a_fb", a_width), ("a_i", a_width), ("a_g", a_width),
             ("b_q", b_width), ("b_k", kvh * LANE), ("b_v", kvh * LANE), ("c_u", c_width), ("c_v", c_width),
             ("gates", 3 * d)]
    col, acc = {}, 0
    for name, width in sizes:
        col[name] = acc
        acc += width
    assert acc == w_in.shape[2]

    n_ctx = n_ctx_seq * ctx_len
    rows = _Rows(n_ctx, n_lat_seq, lat_len)
    tm_proj = min(1024, n_ctx, lat_len)
    tn_proj = min(1024, math.gcd(w_in.shape[2], 1024))
    tm_row = min(256, n_ctx, lat_len)
    tm_moe = 256
    tq_lat = min(256, lat_len)

    cond = jnp.concatenate([c_ctx[None, :], c, jnp.zeros((COND_ROWS - 1 - n_lat_seq, d), F32)], axis=0)
    mod = _modulation(cond, w_mod, b_mod).reshape(depth, COND_ROWS * N_MOD, 1, d)

    p_lb = jax.nn.softmax(hgrn_lb.astype(F32), axis=1)
    lbs = jnp.cumsum(p_lb, axis=1) - p_lb[:, :1]

    cos, sin = _rope_tables(lat_len)
    cache_k2 = cache_k.reshape(n_lat_seq, depth, past, kvh * LANE)
    cache_v2 = cache_v.reshape(n_lat_seq, depth, past, kvh * LANE)
    s0_ctx = jnp.zeros((n_ctx_seq,) + state_hgrn.shape[2:], F32)
    r_pad = jnp.zeros((d, LANE), F32).at[:, :n_experts].set(router_w)
    r_hi, r_lo = _split2(r_pad)
    gmlp_bias = jnp.broadcast_to(gmlp_bs[:, :, :, None], gmlp_bs.shape + (LANE,))

    x = jnp.concatenate([x_prompt.reshape(n_ctx, d), x_sample.reshape(n_lat_seq * lat_len, d)], axis=0)
    moe = None
    new_k, new_v, new_s = [], [], []
    for l in range(depth):
        mod_l = mod[l]
        p_all, x = _in_proj(rows, x, moe, mod[l - 1] if l else None, mod_l, norm1[l], w_in[l].astype(BF16),
                            tm_proj, tn_proj)
        yb_ctx, k_l, v_l = _attention(p_all, col, 0, n_ctx_seq, ctx_len, b_heads, kvh, q_norm[l], k_norm[l],
                                      ctx_len)
        (yb_lat,) = _attention(p_all, col, n_ctx, n_lat_seq, lat_len, b_heads, kvh, q_norm[l], k_norm[l],
                               tq_lat, ctx=(cache_k2, cache_v2, l, cos, sin))
        ya_ctx, s_l = _hgrn(p_all, col, 0, n_ctx_seq, ctx_len, a_heads, lbs[:, l], hgrn_norm[l], s0_ctx)
        ya_lat, _ = _hgrn(p_all, col, n_ctx, n_lat_seq, lat_len, a_heads, lbs[:, l], hgrn_norm[l],
                          state_hgrn[:, l])
        y_c = _gmlp(p_all, col, gmlp_norm[l], gmlp_ws[l].astype(BF16), gmlp_bias[l], tm_proj)
        x, h2, route, counts = _merge(rows, x, jnp.concatenate([ya_ctx, ya_lat], axis=0),
                                      jnp.concatenate([yb_ctx, yb_lat], axis=0), y_c, p_all, col, mod_l,
                                      norm2[l], w_pa[l].astype(BF16), w_pb[l].astype(BF16),
                                      w_pc[l].astype(BF16), w_out[l].astype(BF16), r_hi, r_lo, router_b, tm_row)
        slot_tok, dest, block_e, block_valid = _dispatch(
            route[0:2].astype(jnp.int32), route[4:6].astype(jnp.int32), counts[:, 0].astype(jnp.int32), tm_moe)
        yb = _moe_ffn(jnp.take(h2, slot_tok, axis=0), block_e, block_valid, exp_gate[l].astype(BF16),
                      exp_up[l].astype(BF16), exp_down[l].astype(BF16), tm_moe)
        moe = (jnp.take(yb, dest[0], axis=0).astype(F32) * route[2][:, None]
               + jnp.take(yb, dest[1], axis=0).astype(F32) * route[3][:, None]).astype(BF16)
        new_k.append(k_l)
        new_v.append(v_l)
        new_s.append(s_l)

    y_ctx = _final_norm(rows, x, moe, mod[depth - 1], norm_f, 0, n_ctx, tm_row)
    y_lat = _final_norm(rows, x, moe, mod[depth - 1], norm_f, n_ctx, n_lat_seq * lat_len, tm_row)
    kv_shape = (n_ctx_seq, depth, ctx_len, kvh, LANE)
    return (y_ctx.reshape(n_ctx_seq, ctx_len, d), y_lat.reshape(n_lat_seq, lat_len, d),
            jnp.stack(new_k, axis=1).reshape(kv_shape), jnp.stack(new_v, axis=1).reshape(kv_shape),
            jnp.stack(new_s, axis=1))
```

```python
import functools
import math

import jax
import jax.numpy as jnp
import numpy as np
from jax import lax
from jax.experimental import pallas as pl
from jax.experimental.pallas import tpu as pltpu

F32 = jnp.float32
BF16 = jnp.bfloat16

EPS = 1e-6
N_MOD = 6
GRID_W = 64
ROPE_THETA = 10000.0
MIX_CHUNK = 128
N_GROUPS = 4
TOP_K = 2
LANE = 128
SUBLANES = 8
HGRN_BLOCK = 128
HGRN_LEVELS = 7
ATTN_KEY_CHUNK = 1024
LOG2E = 1.4426950408889634
COND_ROWS = 16
VMEM_LIMIT = 56 * 1024 * 1024

NT_DIMS = (((1,), (1,)), ((), ()))
TN_DIMS = (((0,), (0,)), ((), ()))


def _sigmoid(x):
    return 1.0 / (1.0 + jnp.exp(-x))


def _silu(x):
    return x * _sigmoid(x)


def _gelu_tanh(x):
    return 0.5 * x * (1.0 + jnp.tanh(math.sqrt(2.0 / math.pi) * (x + 0.044715 * (x * x * x))))


def _rms(x):
    return x * lax.rsqrt(jnp.mean(x * x, axis=-1, keepdims=True) + EPS)


def _split2(x):
    hi = x.astype(BF16)
    lo = (x - hi.astype(F32)).astype(BF16)
    return hi, lo


def _split3(x):
    hi = x.astype(BF16)
    r = x - hi.astype(F32)
    mid = r.astype(BF16)
    lo = (r - mid.astype(F32)).astype(BF16)
    return hi, mid, lo


def _params(sem):
    return pltpu.CompilerParams(dimension_semantics=sem, vmem_limit_bytes=VMEM_LIMIT)


def _mod_kernel(c_ref, w_ref, b_ref, o_ref):
    s = _silu(c_ref[...]).astype(BF16)
    o_ref[...] = jnp.dot(s, w_ref[...].astype(BF16), preferred_element_type=F32) + b_ref[...]


def _modulation(cond, w_mod, b_mod):
    depth, d, n = w_mod.shape
    tn = math.gcd(n, 1024)
    return pl.pallas_call(
        _mod_kernel,
        out_shape=jax.ShapeDtypeStruct((depth, COND_ROWS, n), F32),
        grid=(depth, n // tn),
        in_specs=[
            pl.BlockSpec((COND_ROWS, d), lambda l, j: (0, 0)),
            pl.BlockSpec((None, d, tn), lambda l, j: (l, 0, j)),
            pl.BlockSpec((None, 1, tn), lambda l, j: (l, 0, j)),
        ],
        out_specs=pl.BlockSpec((None, COND_ROWS, tn), lambda l, j: (l, 0, j)),
        compiler_params=_params(("parallel", "parallel")),
        name="adaln_modulation",
    )(cond, w_mod, b_mod.reshape(depth, 1, n))


class _Rows:
    def __init__(self, n_ctx, n_lat_seq, lat_len):
        self.n_ctx, self.lat_len = n_ctx, lat_len
        self.total = n_ctx + n_lat_seq * lat_len

    def cond_row(self, i, tm):
        ctx_tiles = self.n_ctx // tm
        per_seq = self.lat_len // tm
        return jnp.where(i < ctx_tiles, 0, 1 + (i - ctx_tiles) // per_seq)

    def mod_spec(self, k, tm, d):
        return pl.BlockSpec((None, 1, d), lambda i, *_: (self.cond_row(i, tm) * N_MOD + k, 0, 0))


def _in_proj_kernel(*refs, combine):
    if combine:
        x_ref, moe_ref, g2_ref, n_ref, sh_ref, sc_ref, w_ref, o_ref, x2_ref, h_scr = refs
    else:
        x_ref, n_ref, sh_ref, sc_ref, w_ref, o_ref, h_scr = refs

    @pl.when(pl.program_id(1) == 0)
    def _():
        x = x_ref[...]
        if combine:
            x = x + g2_ref[...] * moe_ref[...].astype(F32)
            x2_ref[...] = x
        h = _rms(x) * n_ref[...] * (1.0 + sc_ref[...]) + sh_ref[...]
        h_scr[...] = h.astype(BF16)

    o_ref[...] = jnp.dot(h_scr[...], w_ref[...], preferred_element_type=F32).astype(BF16)


def _in_proj(rows, x, moe, mod_prev, mod_l, norm_g, w_bf, layer, tm, tn):
    t, d = x.shape
    n = w_bf.shape[2]
    combine = moe is not None
    row_spec = pl.BlockSpec((tm, d), lambda i, j: (i, 0))
    row_in_spec = pl.BlockSpec((tm, d), lambda i, j: (i, 0), pipeline_mode=pl.Buffered(1))
    vec_spec = pl.BlockSpec((1, d), lambda i, j: (0, 0))
    in_specs = [row_in_spec]
    args = [x]
    if combine:
        in_specs += [row_in_spec, rows.mod_spec(5, tm, d)]
        args += [moe, mod_prev]
    in_specs += [vec_spec, rows.mod_spec(0, tm, d), rows.mod_spec(1, tm, d),
                 pl.BlockSpec((None, d, tn), lambda i, j: (layer, 0, j))]
    args += [norm_g.reshape(1, d), mod_l, mod_l, w_bf]
    out_shape = [jax.ShapeDtypeStruct((t, n), BF16)]
    out_specs = [pl.BlockSpec((tm, tn), lambda i, j: (i, j))]
    if combine:
        out_shape.append(jax.ShapeDtypeStruct((t, d), F32))
        out_specs.append(row_spec)
    res = pl.pallas_call(
        functools.partial(_in_proj_kernel, combine=combine),
        out_shape=out_shape,
        grid=(t // tm, n // tn),
        in_specs=in_specs,
        out_specs=out_specs,
        scratch_shapes=[pltpu.VMEM((tm, d), BF16)],
        compiler_params=_params(("parallel", "arbitrary")),
        name="in_proj",
    )(*args)
    return (res[0], res[1]) if combine else (res[0], x)


def _rope(x, cos, sin_signed, swap):
    swapped = jnp.dot(x.astype(BF16), swap, preferred_element_type=F32)
    return x * cos + swapped * sin_signed


def _attn_kernel(*refs, latent, past, tq, grp, scale):
    if latent:
        (q_ref, kn_ref, vn_ref, ck_ref, cv_ref, cos_ref, sin_ref, swap_ref, qg_ref, kg_ref,
         o_ref, k_scr, v_scr) = refs
    else:
        q_ref, kn_ref, vn_ref, qg_ref, kg_ref, o_ref, ko_ref, vo_ref, k_scr, v_scr = refs
    qi = pl.program_id(2)

    @pl.when(qi == 0)
    def _():
        k = _rms(kn_ref[...].astype(F32)) * kg_ref[...]
        if latent:
            k = _rope(k, cos_ref[...], sin_ref[...], swap_ref[...])
            k_scr[0:past, :] = ck_ref[...].astype(BF16)
            v_scr[0:past, 0:LANE] = cv_ref[...].astype(BF16)
        else:
            ko_ref[...] = k
            vo_ref[...] = vn_ref[...].astype(F32)
        k_scr[past:, :] = k.astype(BF16)
        v_scr[past:, 0:LANE] = vn_ref[...]
        v_scr[:, LANE:] = jnp.ones((v_scr.shape[0], LANE), BF16)

    q = q_ref[...].astype(F32)
    heads = []
    for h in range(grp):
        qh = _rms(q[:, h * LANE:(h + 1) * LANE]) * qg_ref[...]
        if latent:
            r0 = pl.multiple_of(qi * tq, tq)
            qh = _rope(qh, cos_ref[pl.ds(r0, tq), :], sin_ref[pl.ds(r0, tq), :], swap_ref[...])
        heads.append((qh * scale).astype(BF16))
    qs = jnp.concatenate(heads, axis=0)
    n_keys = k_scr.shape[0]
    chunks = [(c0, min(c0 + ATTN_KEY_CHUNK, n_keys)) for c0 in range(0, n_keys, ATTN_KEY_CHUNK)]

    def scores(c):
        return lax.dot_general(qs, k_scr[c[0]:c[1], :], NT_DIMS, preferred_element_type=F32)

    m = acc = None
    s_next = scores(chunks[0])
    for n, (c0, c1) in enumerate(chunks):
        s = s_next
        if n + 1 < len(chunks):
            s_next = scores(chunks[n + 1])
        m_c = jnp.max(functools.reduce(jnp.maximum, [s[:, t:t + LANE] for t in range(0, c1 - c0, LANE)]),
                      axis=-1, keepdims=True)
        m_new = m_c if m is None else jnp.maximum(m, m_c)
        pv = jnp.dot(jnp.exp2(s - m_new).astype(BF16), v_scr[c0:c1, :], preferred_element_type=F32)
        acc = pv if m is None else acc * jnp.exp2(m - m_new) + pv
        m = m_new
    o = acc[:, 0:LANE] / acc[:, LANE:]
    for h in range(grp):
        o_ref[:, h * LANE:(h + 1) * LANE] = o[h * tq:(h + 1) * tq, :].astype(BF16)


def _attention(p_all, col, row0, n_seq, seq_len, n_heads, kvh, q_gain, k_gain, tq, ctx=None):
    grp = n_heads // kvh
    latent = ctx is not None
    qw = grp * LANE
    assert col["b_q"] % qw == 0 and row0 % seq_len == 0 and seq_len % tq == 0
    seq0, nq = row0 // seq_len, seq_len // tq
    q_spec = pl.BlockSpec((tq, qw), lambda b, g, i: (seq0 * nq + b * nq + i, col["b_q"] // qw + g))
    k_spec = pl.BlockSpec((seq_len, LANE), lambda b, g, i: (seq0 + b, col["b_k"] // LANE + g))
    v_spec = pl.BlockSpec((seq_len, LANE), lambda b, g, i: (seq0 + b, col["b_v"] // LANE + g))
    gain_spec = pl.BlockSpec((1, LANE), lambda b, g, i: (0, 0))
    in_specs = [q_spec, k_spec, v_spec]
    args = [p_all, p_all, p_all]
    past = 0
    if latent:
        cache_k, cache_v, layer, cos, sin, swap = ctx
        past = cache_k.shape[2]
        c_spec = pl.BlockSpec((None, None, past, LANE), lambda b, g, i: (b, layer, 0, g))
        t_spec = pl.BlockSpec((seq_len, LANE), lambda b, g, i: (0, 0))
        in_specs += [c_spec, c_spec, t_spec, t_spec, pl.BlockSpec((LANE, LANE), lambda b, g, i: (0, 0))]
        args += [cache_k, cache_v, cos, sin, swap]
    in_specs += [gain_spec, gain_spec]
    args += [q_gain.reshape(1, LANE), k_gain.reshape(1, LANE)]
    out_shape = [jax.ShapeDtypeStruct((n_seq * seq_len, n_heads * LANE), BF16)]
    out_specs = [pl.BlockSpec((tq, qw), lambda b, g, i: (b * nq + i, g))]
    if not latent:
        kv_shape = jax.ShapeDtypeStruct((n_seq, seq_len, kvh * LANE), F32)
        kv_spec = pl.BlockSpec((None, seq_len, LANE), lambda b, g, i: (b, 0, g))
        out_shape += [kv_shape, kv_shape]
        out_specs += [kv_spec, kv_spec]
    return pl.pallas_call(
        functools.partial(_attn_kernel, latent=latent, past=past, tq=tq, grp=grp, scale=LANE ** -0.5 * LOG2E),
        out_shape=out_shape,
        grid=(n_seq, kvh, nq),
        in_specs=in_specs,
        out_specs=out_specs,
        scratch_shapes=[pltpu.VMEM((past + seq_len, LANE), BF16), pltpu.VMEM((past + seq_len, 2 * LANE), BF16)],
        compiler_params=_params(("parallel", "parallel", "arbitrary")),
        name="attention_latent" if latent else "attention_context",
    )(*args)


def _rope_tables(seq_len):
    quarter = LANE // 4
    inv = ROPE_THETA ** (-jnp.arange(quarter, dtype=F32) / quarter)
    pos = jnp.arange(seq_len)
    row = (pos // GRID_W).astype(F32)[:, None] * inv[None, :]
    colp = (pos % GRID_W).astype(F32)[:, None] * inv[None, :]
    cos = jnp.concatenate([jnp.cos(row), jnp.cos(row), jnp.cos(colp), jnp.cos(colp)], axis=-1)
    sin = jnp.concatenate([-jnp.sin(row), jnp.sin(row), -jnp.sin(colp), jnp.sin(colp)], axis=-1)
    lane = np.arange(LANE)
    partner = np.where((lane // quarter) % 2 == 0, lane + quarter, lane - quarter)
    swap = jnp.asarray(lane[:, None] == partner[None, :], BF16)
    return cos, sin, swap


def _hgrn_level_codes():
    r = np.arange(HGRN_BLOCK)[:, None]
    u = np.arange(HGRN_BLOCK)[None, :]
    diff = r ^ u
    code = np.where(diff == 0, 0, 1 + np.floor(np.log2(np.maximum(diff, 1))).astype(np.int32))
    return [jnp.asarray(np.where(keep, code, -1), jnp.int32) for keep in (r >= u, r <= u)]


def _segment_runs(x):
    n_tiles = x.shape[0] // SUBLANES
    sub = lax.broadcasted_iota(jnp.int32, (SUBLANES, x.shape[1]), 0)
    pre_t, suf_t = [], []
    for i in range(n_tiles):
        p = s = x[i * SUBLANES:(i + 1) * SUBLANES]
        pre_i, suf_i = [], []
        n = 1
        while n < SUBLANES:
            pos = sub % (2 * n)
            add_p, add_s = jnp.zeros_like(p), jnp.zeros_like(s)
            for j in range(n):
                add_p = jnp.where(pos == n + j, pltpu.roll(p, j + 1, 0), add_p)
                add_s = jnp.where(pos == n - 1 - j, pltpu.roll(s, SUBLANES - (j + 1), 0), add_s)
            p, s = p + add_p, s + add_s
            pre_i.append(p)
            suf_i.append(s)
            n *= 2
        pre_t.append(pre_i)
        suf_t.append(suf_i)
    levels = len(pre_t[0])
    pre = [[t[m] for t in pre_t] for m in range(levels)]
    suf = [[t[m] for t in suf_t] for m in range(levels)]
    cur_p, cur_s = pre[-1], suf[-1]
    tot = [jnp.broadcast_to(t[SUBLANES - 1:SUBLANES, :], t.shape) for t in cur_p]
    seg = 1
    while seg < n_tiles:
        new_p, new_s, new_tot = list(cur_p), list(cur_s), list(tot)
        for base in range(0, n_tiles, 2 * seg):
            t_first, t_second = tot[base], tot[base + seg]
            both = t_first + t_second
            for i in range(base, base + seg):
                new_s[i] = cur_s[i] + t_second
                new_p[i + seg] = cur_p[i + seg] + t_first
                new_tot[i] = new_tot[i + seg] = both
        cur_p, cur_s, tot = new_p, new_s, new_tot
        pre.append(cur_p)
        suf.append(cur_s)
        seg *= 2
    join = lambda tiles: jnp.concatenate(tiles, axis=0)
    return [join(t) for t in pre], [join(t) for t in suf]


def _hgrn_kernel(q_ref, ff_ref, fb_ref, i_ref, g_ref, lb_ref, ng_ref, s0_ref, lvlf_ref, lvlb_ref, y_ref,
                 sfin_ref, of_scr, ob_scr, *, seq_len):
    blk, n_lvl = HGRN_BLOCK, HGRN_LEVELS
    n_blk = seq_len // blk

    def scan_blocks(chains):
        q, k, v, q_run, k_run = [], [], [], [], []
        for r0, f_ref, lb, _, forward, _, _ in chains:
            q.append(_silu(q_ref[pl.ds(r0, blk), :].astype(F32)))
            g = lb + (1.0 - lb) * _sigmoid(f_ref[pl.ds(r0, blk), :].astype(F32))
            k.append(1.0 - g)
            v.append(i_ref[pl.ds(r0, blk), :])
            log_g = jnp.log(g) * LOG2E
            pre, suf = _segment_runs(log_g)
            near, far = (pre, suf) if forward else (suf, pre)
            q_run.append([log_g] + near)
            k_run.append([None] + [x - log_g for x in far])
        kb = [x.astype(BF16) for x in k]
        level = [c[5][...] for c in chains]
        score = [jnp.where(lv == 0, lax.dot_general(qc.astype(BF16), kc, NT_DIMS, preferred_element_type=F32), 0.0)
                 for lv, qc, kc in zip(level, q, kb)]
        for m in range(n_lvl):
            for i in range(len(chains)):
                qs = (q[i] * jnp.exp2(q_run[i][m])).astype(BF16)
                ks = kb[i] if m == 0 else (k[i] * jnp.exp2(k_run[i][m])).astype(BF16)
                score[i] = jnp.where(level[i] == m + 1,
                                     lax.dot_general(qs, ks, NT_DIMS, preferred_element_type=F32), score[i])
        out = []
        for i, (_, _, _, st, _, _, edge) in enumerate(chains):
            q_all = jnp.exp2(q_run[i][n_lvl])
            o = jnp.dot(score[i].astype(BF16), v[i], preferred_element_type=F32)
            o += lax.dot_general((q[i] * q_all).astype(BF16), st.astype(BF16), NT_DIMS,
                                 preferred_element_type=F32)
            kd = (k[i] * jnp.exp2(k_run[i][n_lvl])).astype(BF16)
            st = st * q_all[edge:edge + 1, :] + lax.dot_general(v[i], kd, TN_DIMS, preferred_element_type=F32)
            out.append((o, st))
        return out

    lb_f, lb_b = lb_ref[0:1, :], lb_ref[1:2, :]

    def body(j, carry):
        st_f, st_b = carry
        r_f = pl.multiple_of(j * blk, blk)
        r_b = pl.multiple_of((n_blk - 1 - j) * blk, blk)
        (o_f, st_f), (o_b, st_b) = scan_blocks([(r_f, ff_ref, lb_f, st_f, True, lvlf_ref, blk - 1),
                                                (r_b, fb_ref, lb_b, st_b, False, lvlb_ref, 0)])
        of_scr[pl.ds(r_f, blk), :] = o_f
        ob_scr[pl.ds(r_b, blk), :] = o_b
        return st_f, st_b

    st_f, st_b = lax.fori_loop(0, n_blk, body, (s0_ref[0].T, s0_ref[1].T))
    sfin_ref[0] = st_f.T
    sfin_ref[1] = st_b.T
    o = _rms(of_scr[...] + ob_scr[...]) * ng_ref[...]
    y_ref[...] = (o * _silu(g_ref[...].astype(F32))).astype(BF16)


def _hgrn(p_all, col, row0, n_seq, seq_len, n_heads, lb_l, norm_g, s0):
    assert row0 % seq_len == 0 and seq_len % HGRN_BLOCK == 0
    seq0 = row0 // seq_len

    def cspec(name):
        return pl.BlockSpec((seq_len, LANE), lambda b, h: (seq0 + b, col[name] // LANE + h))

    st_spec = pl.BlockSpec((None, 2, None, LANE, LANE), lambda b, h: (b, 0, h, 0, 0))
    tables = _hgrn_level_codes()
    return pl.pallas_call(
        functools.partial(_hgrn_kernel, seq_len=seq_len),
        out_shape=[jax.ShapeDtypeStruct((n_seq * seq_len, n_heads * LANE), BF16),
                   jax.ShapeDtypeStruct(s0.shape, F32)],
        grid=(n_seq, n_heads),
        in_specs=[cspec("a_q"), cspec("a_ff"), cspec("a_fb"), cspec("a_i"), cspec("a_g"),
                  pl.BlockSpec((2, LANE), lambda b, h: (0, h)),
                  pl.BlockSpec((1, LANE), lambda b, h: (0, 0)),
                  st_spec] + [pl.BlockSpec(t.shape, lambda b, h: (0, 0)) for t in tables],
        out_specs=[pl.BlockSpec((seq_len, LANE), lambda b, h: (b, h)), st_spec],
        scratch_shapes=[pltpu.VMEM((seq_len, LANE), F32), pltpu.VMEM((seq_len, LANE), F32)],
        compiler_params=_params(("parallel", "parallel")),
        name="hgrn_%d" % seq_len,
    )(p_all, p_all, p_all, p_all, p_all, lb_l, norm_g.reshape(1, LANE), s0, *tables)


def _gmlp_kernel(u_ref, v_ref, gn_ref, ws_ref, bs_ref, o_ref, *, n_chunks, groups):
    u = _gelu_tanh(u_ref[...].astype(F32))
    v = (_rms(_gelu_tanh(v_ref[...].astype(F32))) * gn_ref[...]).astype(BF16)
    for n in range(n_chunks):
        rs = slice(n * MIX_CHUNK, (n + 1) * MIX_CHUNK)
        for g in range(groups):
            cs = slice(g * LANE, (g + 1) * LANE)
            mixed = jnp.dot(ws_ref[g], v[rs, cs], preferred_element_type=F32) + bs_ref[g]
            o_ref[rs, cs] = (u[rs, cs] * mixed).astype(BF16)


def _gmlp(p_all, col, norm_g, ws_bf, bias, tm):
    t = p_all.shape[0]
    groups = ws_bf.shape[0]
    cw = groups * LANE
    assert col["c_u"] % cw == 0 and col["c_v"] % cw == 0
    return pl.pallas_call(
        functools.partial(_gmlp_kernel, n_chunks=tm // MIX_CHUNK, groups=groups),
        out_shape=jax.ShapeDtypeStruct((t, cw), BF16),
        grid=(t // tm,),
        in_specs=[pl.BlockSpec((tm, cw), lambda i: (i, col["c_u"] // cw)),
                  pl.BlockSpec((tm, cw), lambda i: (i, col["c_v"] // cw)),
                  pl.BlockSpec((1, cw), lambda i: (0, 0)),
                  pl.BlockSpec((groups, MIX_CHUNK, MIX_CHUNK), lambda i: (0, 0, 0)),
                  pl.BlockSpec((groups, MIX_CHUNK, LANE), lambda i: (0, 0, 0))],
        out_specs=pl.BlockSpec((tm, cw), lambda i: (i, 0)),
        compiler_params=_params(("parallel",)),
        name="gmlp",
    )(p_all, p_all, norm_g.reshape(1, cw), ws_bf, bias)


def _route_rows(logit_rows):
    per = len(logit_rows) // N_GROUPS
    m = functools.reduce(jnp.maximum, logit_rows)
    p = [jnp.exp(r - m) for r in logit_rows]

    def top2_sum(a):
        pairs = [a[i] + a[j] for i in range(len(a)) for j in range(i + 1, len(a))]
        return functools.reduce(jnp.maximum, pairs)

    scores = [top2_sum(p[g * per:(g + 1) * per]) for g in range(N_GROUPS)]
    best, gi = scores[0], jnp.zeros(m.shape, jnp.int32)
    for g in range(1, N_GROUPS):
        better = scores[g] > best
        gi = jnp.where(better, g, gi)
        best = jnp.where(better, scores[g], best)
    a = []
    for j in range(per):
        aj = p[(N_GROUPS - 1) * per + j]
        for g in reversed(range(N_GROUPS - 1)):
            aj = jnp.where(gi == g, p[g * per + j], aj)
        a.append(aj)
    v1, i1 = a[0], jnp.zeros(m.shape, jnp.int32)
    for j in range(1, per):
        better = a[j] > v1
        i1 = jnp.where(better, j, i1)
        v1 = jnp.where(better, a[j], v1)
    v2, i2 = jnp.full(m.shape, -1.0, F32), jnp.zeros(m.shape, jnp.int32)
    for j in range(per):
        better = (i1 != j) & (a[j] > v2)
        i2 = jnp.where(better, j, i2)
        v2 = jnp.where(better, a[j], v2)
    inv = 1.0 / (v1 + v2)
    return gi * per + i1, gi * per + i2, v1 * inv, v2 * inv


def _merge_kernel(x_ref, ya_ctx_ref, ya_lat_ref, yb_ctx_ref, yb_lat_ref, yc_ref, ga0, ga1, gb0, gb1, gc0, gc1,
                  wpa_ref, wpb_ref, wpc_ref, wout_ref, g1_ref, n2_ref, sh2_ref, sc2_ref, rhi_ref, rlo_ref, rb_ref,
                  x1_ref, h2_ref, rt_ref, cnt_ref, *, n_experts, ctx_tiles):
    def gate(r0, r1):
        return jnp.concatenate([_sigmoid(r0[...].astype(F32)), _sigmoid(r1[...].astype(F32))], axis=-1)

    is_ctx = pl.program_id(0) < ctx_tiles
    ya = jnp.where(is_ctx, ya_ctx_ref[...], ya_lat_ref[...])
    yb = jnp.where(is_ctx, yb_ctx_ref[...], yb_lat_ref[...])
    merged = gate(ga0, ga1) * jnp.dot(ya, wpa_ref[...], preferred_element_type=F32)
    merged += gate(gb0, gb1) * jnp.dot(yb, wpb_ref[...], preferred_element_type=F32)
    merged += gate(gc0, gc1) * jnp.dot(yc_ref[...], wpc_ref[...], preferred_element_type=F32)
    out = jnp.dot(merged.astype(BF16), wout_ref[...], preferred_element_type=F32)
    x1 = x_ref[...] + g1_ref[...] * out
    x1_ref[...] = x1
    h2 = _rms(x1) * n2_ref[...] * (1.0 + sc2_ref[...]) + sh2_ref[...]
    h2_ref[...] = h2.astype(BF16)
    hi, lo = _split2(h2)
    rhi, rlo = rhi_ref[...], rlo_ref[...]
    logits = (jnp.dot(hi, rhi, preferred_element_type=F32) + jnp.dot(hi, rlo, preferred_element_type=F32)
              + jnp.dot(lo, rhi, preferred_element_type=F32))
    lt = logits.T[0:n_experts, :] + rb_ref[...]
    tm = lt.shape[1]
    e1, e2, w1, w2 = _route_rows([lt[e:e + 1, :] for e in range(n_experts)])

    @pl.when(pl.program_id(0) == 0)
    def _():
        cnt_ref[...] = jnp.zeros_like(cnt_ref)

    e_id = lax.broadcasted_iota(jnp.int32, (n_experts, tm), 0)
    hit1, hit2 = e_id == e1, e_id == e2
    onehot = (hit1 | hit2).astype(BF16)
    src = lax.broadcasted_iota(jnp.int32, (tm, tm), 0)
    dst = lax.broadcasted_iota(jnp.int32, (tm, tm), 1)
    before = jnp.dot(onehot, (src < dst).astype(BF16), preferred_element_type=F32) + cnt_ref[...]
    rank1 = jnp.sum(jnp.where(hit1, before, 0.0), axis=0, keepdims=True)
    rank2 = jnp.sum(jnp.where(hit2, before, 0.0), axis=0, keepdims=True)
    cnt_ref[...] += jnp.dot(onehot, jnp.ones((tm, tm), BF16), preferred_element_type=F32)
    zero = jnp.zeros_like(w1)
    rt_ref[...] = jnp.concatenate([e1.astype(F32), e2.astype(F32), w1, w2, rank1, rank2, zero, zero], axis=0)


ROUTE_ROWS = 8


def _merge(rows, x, y_a, y_b, y_c, p_all, col, mod_l, norm2_g, w_pa, w_pb, w_pc, w_out, layer, r_hi, r_lo,
           r_bias, tm):
    t, d = x.shape
    n_experts = r_bias.shape[0]
    gw = d // 2
    assert col["gates"] % gw == 0
    g0 = col["gates"] // gw
    ctx_tiles = rows.n_ctx // tm
    row = lambda w: pl.BlockSpec((tm, w), lambda i: (i, 0))
    ctx_row = lambda w: pl.BlockSpec((tm, w), lambda i: (jnp.minimum(i, ctx_tiles - 1), 0))
    lat_row = lambda w: pl.BlockSpec((tm, w), lambda i: (jnp.maximum(i - ctx_tiles, 0), 0))
    full = lambda a: pl.BlockSpec(a.shape, lambda i: (0,) * a.ndim, pipeline_mode=pl.Buffered(1))
    of_layer = lambda a: pl.BlockSpec((None,) + a.shape[1:], lambda i: (layer,) + (0,) * (a.ndim - 1),
                                      pipeline_mode=pl.Buffered(1))
    gate_specs = [pl.BlockSpec((tm, gw), functools.partial(lambda i, k: (i, g0 + k), k=k)) for k in range(6)]
    return pl.pallas_call(
        functools.partial(_merge_kernel, n_experts=n_experts, ctx_tiles=ctx_tiles),
        out_shape=[jax.ShapeDtypeStruct((t, d), F32), jax.ShapeDtypeStruct((t, d), BF16),
                   jax.ShapeDtypeStruct((ROUTE_ROWS, t), F32), jax.ShapeDtypeStruct((n_experts, tm), F32)],
        grid=(t // tm,),
        in_specs=[row(d), ctx_row(y_a[0].shape[1]), lat_row(y_a[1].shape[1]), ctx_row(y_b[0].shape[1]),
                  lat_row(y_b[1].shape[1]), row(y_c.shape[1])] + gate_specs
                 + [of_layer(w_pa), of_layer(w_pb), of_layer(w_pc), of_layer(w_out), rows.mod_spec(2, tm, d),
                    pl.BlockSpec((1, d), lambda i: (0, 0)), rows.mod_spec(3, tm, d), rows.mod_spec(4, tm, d),
                    full(r_hi), full(r_lo), pl.BlockSpec((n_experts, tm), lambda i: (0, 0))],
        out_specs=[row(d), row(d), pl.BlockSpec((ROUTE_ROWS, tm), lambda i: (0, i)),
                   pl.BlockSpec((n_experts, tm), lambda i: (0, 0))],
        compiler_params=_params(("arbitrary",)),
        name="merge_out_proj",
    )(x, *y_a, *y_b, y_c, *([p_all] * 6), w_pa, w_pb, w_pc, w_out, mod_l, norm2_g.reshape(1, d), mod_l, mod_l,
      r_hi, r_lo, jnp.broadcast_to(r_bias.astype(F32)[:, None], (n_experts, tm)))


def _moe_kernel(be_ref, bv_ref, x_ref, wg_ref, wu_ref, wd_ref, o_ref):
    i = pl.program_id(0)

    @pl.when(bv_ref[i] > 0)
    def _():
        x = x_ref[...]
        a = jnp.dot(x, wg_ref[...], preferred_element_type=F32)
        b = jnp.dot(x, wu_ref[...], preferred_element_type=F32)
        h = (_silu(a) * b).astype(BF16)
        o_ref[...] = jnp.dot(h, wd_ref[...], preferred_element_type=F32).astype(BF16)

    @pl.when(bv_ref[i] == 0)
    def _():
        o_ref[...] = jnp.zeros_like(o_ref)


def _moe_ffn(xb, block_e, block_valid, wg, wu, wd, layer, tm):
    n_slots, d = xb.shape
    ff = wg.shape[3]
    grid_spec = pltpu.PrefetchScalarGridSpec(
        num_scalar_prefetch=2,
        grid=(n_slots // tm,),
        in_specs=[pl.BlockSpec((tm, d), lambda i, be, bv: (i, 0)),
                  pl.BlockSpec((None, None, d, ff), lambda i, be, bv: (layer, be[i], 0, 0)),
                  pl.BlockSpec((None, None, d, ff), lambda i, be, bv: (layer, be[i], 0, 0)),
                  pl.BlockSpec((None, None, ff, d), lambda i, be, bv: (layer, be[i], 0, 0))],
        out_specs=pl.BlockSpec((tm, d), lambda i, be, bv: (i, 0)),
    )
    return pl.pallas_call(
        _moe_kernel,
        out_shape=jax.ShapeDtypeStruct((n_slots, d), BF16),
        grid_spec=grid_spec,
        compiler_params=_params(("arbitrary",)),
        name="moe_experts",
    )(block_e, block_valid, xb, wg, wu, wd)


def _dispatch(expert, rank, counts, tm):
    n_experts = counts.shape[0]
    n_tok = expert.shape[1]
    n_assign = TOP_K * n_tok
    padded = (counts + tm - 1) // tm * tm
    pad_end = jnp.cumsum(padded)
    pad_start = pad_end - padded
    first_slot = jnp.sum(jnp.where(expert[..., None] == jnp.arange(n_experts), pad_start, 0), axis=-1)
    dest = first_slot + rank
    n_blocks = -(-(n_assign + n_experts * (tm - 1)) // tm)
    tok = jnp.broadcast_to(jnp.arange(n_tok, dtype=jnp.int32)[None, :], dest.shape)
    slot_tok = jnp.zeros((n_blocks * tm,), jnp.int32).at[dest.reshape(-1)].set(tok.reshape(-1))
    block_start = jnp.arange(n_blocks, dtype=jnp.int32) * tm
    block_e = jnp.minimum(jnp.sum(block_start[:, None] >= pad_end[None, :], axis=1), n_experts - 1)
    block_valid = (block_start < pad_end[-1]).astype(jnp.int32)
    return slot_tok, dest, block_e.astype(jnp.int32), block_valid


def _final_kernel(x_ref, moe_ref, g2_ref, nf_ref, o_ref):
    x = x_ref[...] + g2_ref[...] * moe_ref[...].astype(F32)
    o_ref[...] = _rms(x) * nf_ref[...]


def _final_norm(rows, x, moe, mod_l, norm_f, row0, n_rows, tm):
    d = x.shape[1]
    t0 = row0 // tm
    row_spec = pl.BlockSpec((tm, d), lambda i: (t0 + i, 0))
    g2_spec = pl.BlockSpec((None, 1, d), lambda i: (rows.cond_row(t0 + i, tm) * N_MOD + 5, 0, 0))
    return pl.pallas_call(
        _final_kernel,
        out_shape=jax.ShapeDtypeStruct((n_rows, d), F32),
        grid=(n_rows // tm,),
        in_specs=[row_spec, row_spec, g2_spec, pl.BlockSpec((1, d), lambda i: (0, 0))],
        out_specs=pl.BlockSpec((tm, d), lambda i: (i, 0)),
        compiler_params=_params(("parallel",)),
        name="final_norm",
    )(x, moe, mod_l, norm_f.reshape(1, d))


def kernel(x_prompt, x_sample, cache_k, cache_v, state_hgrn, c, c_ctx, w_mod, b_mod, norm1, norm2, norm_f,
           w_in, hgrn_lb, hgrn_norm, q_norm, k_norm, gmlp_norm, gmlp_ws, gmlp_bs, w_pa, w_pb, w_pc, w_out,
           router_w, router_b, exp_gate, exp_up, exp_down):
    n_ctx_seq, ctx_len, d = x_prompt.shape
    n_lat_seq, lat_len, _ = x_sample.shape
    depth = w_mod.shape[0]
    past, kvh = cache_k.shape[2], cache_k.shape[3]
    a_heads = state_hgrn.shape[3]
    a_width, b_width, c_width = hgrn_lb.shape[2], w_pb.shape[1], w_pc.shape[1]
    b_heads = b_width // LANE
    n_experts = router_w.shape[1]
    assert cache_k.shape[4] == LANE and state_hgrn.shape[4:] == (LANE, LANE) and a_width == a_heads * LANE

    sizes = [("a_q", a_width), ("a_ff", a_width), ("a_fb", a_width), ("a_i", a_width), ("a_g", a_width),
             ("b_q", b_width), ("b_k", kvh * LANE), ("b_v", kvh * LANE), ("c_u", c_width), ("c_v", c_width),
             ("gates", 3 * d)]
    col, acc = {}, 0
    for name, width in sizes:
        col[name] = acc
        acc += width
    assert acc == w_in.shape[2]

    n_ctx = n_ctx_seq * ctx_len
    rows = _Rows(n_ctx, n_lat_seq, lat_len)
    tm_proj = min(1024, n_ctx, lat_len)
    tn_proj = min(1024, math.gcd(w_in.shape[2], 1024))
    tm_row = min(256, n_ctx, lat_len)
    tm_moe = 256
    tq_lat = min(256, lat_len)

    cond = jnp.concatenate([c_ctx[None, :], c, jnp.zeros((COND_ROWS - 1 - n_lat_seq, d), F32)], axis=0)
    mod = _modulation(cond, w_mod, b_mod).reshape(depth, COND_ROWS * N_MOD, 1, d)

    p_lb = jax.nn.softmax(hgrn_lb.astype(F32), axis=1)
    lbs = jnp.cumsum(p_lb, axis=1) - p_lb[:, :1]

    cos, sin, swap = _rope_tables(lat_len)
    cache_k2 = cache_k.reshape(n_lat_seq, depth, past, kvh * LANE)
    cache_v2 = cache_v.reshape(n_lat_seq, depth, past, kvh * LANE)
    s0_ctx = jnp.zeros((n_ctx_seq,) + state_hgrn.shape[2:], F32)
    r_pad = jnp.zeros((d, LANE), F32).at[:, :n_experts].set(router_w)
    r_hi, r_lo = _split2(r_pad)
    gmlp_bias = jnp.broadcast_to(gmlp_bs[:, :, :, None], gmlp_bs.shape + (LANE,))

    w_in_bf, w_pa_bf, w_pb_bf, w_pc_bf, w_out_bf, gate_bf, up_bf, down_bf = (
        w.astype(BF16) for w in (w_in, w_pa, w_pb, w_pc, w_out, exp_gate, exp_up, exp_down))

    x = jnp.concatenate([x_prompt.reshape(n_ctx, d), x_sample.reshape(n_lat_seq * lat_len, d)], axis=0)
    moe = None
    new_k, new_v, new_s = [], [], []
    for l in range(depth):
        mod_l = mod[l]
        p_all, x = _in_proj(rows, x, moe, mod[l - 1] if l else None, mod_l, norm1[l], w_in_bf, l,
                            tm_proj, tn_proj)
        yb_ctx, k_l, v_l = _attention(p_all, col, 0, n_ctx_seq, ctx_len, b_heads, kvh, q_norm[l], k_norm[l],
                                      ctx_len)
        (yb_lat,) = _attention(p_all, col, n_ctx, n_lat_seq, lat_len, b_heads, kvh, q_norm[l], k_norm[l],
                               tq_lat, ctx=(cache_k2, cache_v2, l, cos, sin, swap))
        ya_ctx, s_l = _hgrn(p_all, col, 0, n_ctx_seq, ctx_len, a_heads, lbs[:, l], hgrn_norm[l], s0_ctx)
        ya_lat, _ = _hgrn(p_all, col, n_ctx, n_lat_seq, lat_len, a_heads, lbs[:, l], hgrn_norm[l],
                          state_hgrn[:, l])
        y_c = _gmlp(p_all, col, gmlp_norm[l], gmlp_ws[l].astype(BF16), gmlp_bias[l], tm_proj)
        x, h2, route, counts = _merge(rows, x, (ya_ctx, ya_lat), (yb_ctx, yb_lat), y_c, p_all, col, mod_l,
                                      norm2[l], w_pa_bf, w_pb_bf, w_pc_bf, w_out_bf, l, r_hi, r_lo, router_b,
                                      tm_row)
        slot_tok, dest, block_e, block_valid = _dispatch(
            route[0:2].astype(jnp.int32), route[4:6].astype(jnp.int32), counts[:, 0].astype(jnp.int32), tm_moe)
        yb = _moe_ffn(jnp.take(h2, slot_tok, axis=0), block_e, block_valid, gate_bf, up_bf, down_bf, l, tm_moe)
        moe = (jnp.take(yb, dest[0], axis=0).astype(F32) * route[2][:, None]
               + jnp.take(yb, dest[1], axis=0).astype(F32) * route[3][:, None]).astype(BF16)
        new_k.append(k_l)
        new_v.append(v_l)
        new_s.append(s_l)

    y_ctx = _final_norm(rows, x, moe, mod[depth - 1], norm_f, 0, n_ctx, tm_row)
    y_lat = _final_norm(rows, x, moe, mod[depth - 1], norm_f, n_ctx, n_lat_seq * lat_len, tm_row)
    kv_shape = (n_ctx_seq, depth, ctx_len, kvh, LANE)
    return (y_ctx.reshape(n_ctx_seq, ctx_len, d), y_lat.reshape(n_lat_seq, lat_len, d),
            jnp.stack(new_k, axis=1).reshape(kv_shape), jnp.stack(new_v, axis=1).reshape(kv_shape),
            jnp.stack(new_s, axis=1))
```

```python
import functools
import math

import jax
import jax.numpy as jnp
import numpy as np
from jax import lax
from jax.experimental import pallas as pl
from jax.experimental.pallas import tpu as pltpu

F32 = jnp.float32
BF16 = jnp.bfloat16

EPS = 1e-6
N_MOD = 6
GRID_W = 64
ROPE_THETA = 10000.0
MIX_CHUNK = 128
N_GROUPS = 4
TOP_K = 2
LANE = 128
SUBLANES = 8
HGRN_BLOCK = 128
HGRN_LEVELS = 7
ATTN_KEY_CHUNK = 1024
LOG2E = 1.4426950408889634
COND_ROWS = 16
VMEM_LIMIT = 56 * 1024 * 1024

NT_DIMS = (((1,), (1,)), ((), ()))
TN_DIMS = (((0,), (0,)), ((), ()))


def _sigmoid(x):
    return 1.0 / (1.0 + jnp.exp(-x))


def _silu(x):
    return x * _sigmoid(x)


def _gelu_tanh(x):
    return 0.5 * x * (1.0 + jnp.tanh(math.sqrt(2.0 / math.pi) * (x + 0.044715 * (x * x * x))))


def _rms(x):
    return x * lax.rsqrt(jnp.mean(x * x, axis=-1, keepdims=True) + EPS)


def _split2(x):
    hi = x.astype(BF16)
    lo = (x - hi.astype(F32)).astype(BF16)
    return hi, lo


def _split3(x):
    hi = x.astype(BF16)
    r = x - hi.astype(F32)
    mid = r.astype(BF16)
    lo = (r - mid.astype(F32)).astype(BF16)
    return hi, mid, lo


def _params(sem):
    return pltpu.CompilerParams(dimension_semantics=sem, vmem_limit_bytes=VMEM_LIMIT)


def _mod_kernel(c_ref, w_ref, b_ref, o_ref):
    s = _silu(c_ref[...]).astype(BF16)
    o_ref[...] = jnp.dot(s, w_ref[...].astype(BF16), preferred_element_type=F32) + b_ref[...]


def _modulation(cond, w_mod, b_mod):
    depth, d, n = w_mod.shape
    tn = math.gcd(n, 1024)
    return pl.pallas_call(
        _mod_kernel,
        out_shape=jax.ShapeDtypeStruct((depth, COND_ROWS, n), F32),
        grid=(depth, n // tn),
        in_specs=[
            pl.BlockSpec((COND_ROWS, d), lambda l, j: (0, 0)),
            pl.BlockSpec((None, d, tn), lambda l, j: (l, 0, j)),
            pl.BlockSpec((None, 1, tn), lambda l, j: (l, 0, j)),
        ],
        out_specs=pl.BlockSpec((None, COND_ROWS, tn), lambda l, j: (l, 0, j)),
        compiler_params=_params(("parallel", "parallel")),
        name="adaln_modulation",
    )(cond, w_mod, b_mod.reshape(depth, 1, n))


class _Rows:
    def __init__(self, n_ctx, n_lat_seq, lat_len):
        self.n_ctx, self.lat_len = n_ctx, lat_len
        self.total = n_ctx + n_lat_seq * lat_len

    def cond_row(self, i, tm):
        ctx_tiles = self.n_ctx // tm
        per_seq = self.lat_len // tm
        return jnp.where(i < ctx_tiles, 0, 1 + (i - ctx_tiles) // per_seq)

    def mod_spec(self, k, tm, d):
        return pl.BlockSpec((None, 1, d), lambda i, *_: (self.cond_row(i, tm) * N_MOD + k, 0, 0))


def _in_proj_kernel(*refs, combine):
    if combine:
        x_ref, moe_ref, g2_ref, n_ref, sh_ref, sc_ref, w_ref, o_ref, x2_ref, h_scr = refs
    else:
        x_ref, n_ref, sh_ref, sc_ref, w_ref, o_ref, h_scr = refs

    @pl.when(pl.program_id(1) == 0)
    def _():
        x = x_ref[...]
        if combine:
            x = x + g2_ref[...] * moe_ref[...].astype(F32)
            x2_ref[...] = x
        h = _rms(x) * n_ref[...] * (1.0 + sc_ref[...]) + sh_ref[...]
        h_scr[...] = h.astype(BF16)

    o_ref[...] = jnp.dot(h_scr[...], w_ref[...], preferred_element_type=F32).astype(BF16)


def _in_proj(rows, x, moe, mod_prev, mod_l, norm_g, w_bf, layer, tm, tn):
    t, d = x.shape
    n = w_bf.shape[2]
    combine = moe is not None
    row_spec = pl.BlockSpec((tm, d), lambda i, j: (i, 0))
    row_once_spec = pl.BlockSpec((tm, d), lambda i, j: (i, 0), pipeline_mode=pl.Buffered(1))
    vec_spec = pl.BlockSpec((1, d), lambda i, j: (0, 0))
    in_specs = [row_once_spec]
    args = [x]
    if combine:
        in_specs += [row_once_spec, rows.mod_spec(5, tm, d)]
        args += [moe, mod_prev]
    in_specs += [vec_spec, rows.mod_spec(0, tm, d), rows.mod_spec(1, tm, d),
                 pl.BlockSpec((None, d, tn), lambda i, j: (layer, 0, j))]
    args += [norm_g.reshape(1, d), mod_l, mod_l, w_bf]
    out_shape = [jax.ShapeDtypeStruct((t, n), BF16)]
    out_specs = [pl.BlockSpec((tm, tn), lambda i, j: (i, j))]
    if combine:
        out_shape.append(jax.ShapeDtypeStruct((t, d), F32))
        out_specs.append(row_spec)
    res = pl.pallas_call(
        functools.partial(_in_proj_kernel, combine=combine),
        out_shape=out_shape,
        grid=(t // tm, n // tn),
        in_specs=in_specs,
        out_specs=out_specs,
        scratch_shapes=[pltpu.VMEM((tm, d), BF16)],
        compiler_params=_params(("parallel", "arbitrary")),
        name="in_proj",
    )(*args)
    return (res[0], res[1]) if combine else (res[0], x)


def _rope(x, cos, sin_signed, swap):
    swapped = jnp.dot(x.astype(BF16), swap, preferred_element_type=F32)
    return x * cos + swapped * sin_signed


def _attn_kernel(*refs, latent, past, tq, grp, scale):
    if latent:
        (q_ref, kn_ref, vn_ref, ck_ref, cv_ref, cos_ref, sin_ref, swap_ref, qg_ref, kg_ref,
         o_ref, k_scr, v_scr) = refs
    else:
        q_ref, kn_ref, vn_ref, qg_ref, kg_ref, o_ref, ko_ref, vo_ref, k_scr, v_scr = refs
    qi = pl.program_id(2)

    @pl.when(qi == 0)
    def _():
        k = _rms(kn_ref[...].astype(F32)) * kg_ref[...]
        if latent:
            k = _rope(k, cos_ref[...], sin_ref[...], swap_ref[...])
            k_scr[0:past, :] = ck_ref[...].astype(BF16)
            v_scr[0:past, 0:LANE] = cv_ref[...].astype(BF16)
        else:
            ko_ref[...] = k
            vo_ref[...] = vn_ref[...].astype(F32)
        k_scr[past:, :] = k.astype(BF16)
        v_scr[past:, 0:LANE] = vn_ref[...]
        v_scr[:, LANE:] = jnp.ones((v_scr.shape[0], LANE), BF16)

    q = q_ref[...].astype(F32)
    heads = []
    for h in range(grp):
        qh = _rms(q[:, h * LANE:(h + 1) * LANE]) * qg_ref[...]
        if latent:
            r0 = pl.multiple_of(qi * tq, tq)
            qh = _rope(qh, cos_ref[pl.ds(r0, tq), :], sin_ref[pl.ds(r0, tq), :], swap_ref[...])
        heads.append((qh * scale).astype(BF16))
    qs = jnp.concatenate(heads, axis=0)
    n_keys = k_scr.shape[0]
    chunks = [(c0, min(c0 + ATTN_KEY_CHUNK, n_keys)) for c0 in range(0, n_keys, ATTN_KEY_CHUNK)]

    def scores(c):
        return lax.dot_general(qs, k_scr[c[0]:c[1], :], NT_DIMS, preferred_element_type=F32)

    m = acc = None
    s_next = scores(chunks[0])
    for n, (c0, c1) in enumerate(chunks):
        s = s_next
        if n + 1 < len(chunks):
            s_next = scores(chunks[n + 1])
        m_c = jnp.max(functools.reduce(jnp.maximum, [s[:, t:t + LANE] for t in range(0, c1 - c0, LANE)]),
                      axis=-1, keepdims=True)
        m_new = m_c if m is None else jnp.maximum(m, m_c)
        pv = jnp.dot(jnp.exp2(s - m_new).astype(BF16), v_scr[c0:c1, :], preferred_element_type=F32)
        acc = pv if m is None else acc * jnp.exp2(m - m_new) + pv
        m = m_new
    o = acc[:, 0:LANE] / acc[:, LANE:]
    for h in range(grp):
        o_ref[:, h * LANE:(h + 1) * LANE] = o[h * tq:(h + 1) * tq, :].astype(BF16)


def _attention(p_all, col, row0, n_seq, seq_len, n_heads, kvh, q_gain, k_gain, tq, ctx=None):
    grp = n_heads // kvh
    latent = ctx is not None
    qw = grp * LANE
    assert col["b_q"] % qw == 0 and row0 % seq_len == 0 and seq_len % tq == 0
    seq0, nq = row0 // seq_len, seq_len // tq
    q_spec = pl.BlockSpec((tq, qw), lambda b, g, i: (seq0 * nq + b * nq + i, col["b_q"] // qw + g))
    k_spec = pl.BlockSpec((seq_len, LANE), lambda b, g, i: (seq0 + b, col["b_k"] // LANE + g))
    v_spec = pl.BlockSpec((seq_len, LANE), lambda b, g, i: (seq0 + b, col["b_v"] // LANE + g))
    gain_spec = pl.BlockSpec((1, LANE), lambda b, g, i: (0, 0))
    in_specs = [q_spec, k_spec, v_spec]
    args = [p_all, p_all, p_all]
    past = 0
    if latent:
        cache_k, cache_v, layer, cos, sin, swap = ctx
        past = cache_k.shape[2]
        c_spec = pl.BlockSpec((None, None, past, LANE), lambda b, g, i: (b, layer, 0, g))
        t_spec = pl.BlockSpec((seq_len, LANE), lambda b, g, i: (0, 0))
        in_specs += [c_spec, c_spec, t_spec, t_spec, pl.BlockSpec((LANE, LANE), lambda b, g, i: (0, 0))]
        args += [cache_k, cache_v, cos, sin, swap]
    in_specs += [gain_spec, gain_spec]
    args += [q_gain.reshape(1, LANE), k_gain.reshape(1, LANE)]
    out_shape = [jax.ShapeDtypeStruct((n_seq * seq_len, n_heads * LANE), BF16)]
    out_specs = [pl.BlockSpec((tq, qw), lambda b, g, i: (b * nq + i, g))]
    if not latent:
        kv_shape = jax.ShapeDtypeStruct((n_seq, seq_len, kvh * LANE), F32)
        kv_spec = pl.BlockSpec((None, seq_len, LANE), lambda b, g, i: (b, 0, g))
        out_shape += [kv_shape, kv_shape]
        out_specs += [kv_spec, kv_spec]
    return pl.pallas_call(
        functools.partial(_attn_kernel, latent=latent, past=past, tq=tq, grp=grp, scale=LANE ** -0.5 * LOG2E),
        out_shape=out_shape,
        grid=(n_seq, kvh, nq),
        in_specs=in_specs,
        out_specs=out_specs,
        scratch_shapes=[pltpu.VMEM((past + seq_len, LANE), BF16), pltpu.VMEM((past + seq_len, 2 * LANE), BF16)],
        compiler_params=_params(("parallel", "parallel", "arbitrary")),
        name="attention_latent" if latent else "attention_context",
    )(*args)


def _rope_tables(seq_len):
    quarter = LANE // 4
    inv = ROPE_THETA ** (-jnp.arange(quarter, dtype=F32) / quarter)
    pos = jnp.arange(seq_len)
    row = (pos // GRID_W).astype(F32)[:, None] * inv[None, :]
    colp = (pos % GRID_W).astype(F32)[:, None] * inv[None, :]
    cos = jnp.concatenate([jnp.cos(row), jnp.cos(row), jnp.cos(colp), jnp.cos(colp)], axis=-1)
    sin = jnp.concatenate([-jnp.sin(row), jnp.sin(row), -jnp.sin(colp), jnp.sin(colp)], axis=-1)
    lane = np.arange(LANE)
    partner = np.where((lane // quarter) % 2 == 0, lane + quarter, lane - quarter)
    swap = jnp.asarray(lane[:, None] == partner[None, :], BF16)
    return cos, sin, swap


def _hgrn_level_codes():
    r = np.arange(HGRN_BLOCK)[:, None]
    u = np.arange(HGRN_BLOCK)[None, :]
    diff = r ^ u
    code = np.where(diff == 0, 0, 1 + np.floor(np.log2(np.maximum(diff, 1))).astype(np.int32))
    return [jnp.asarray(np.where(keep, code, -1), jnp.int32) for keep in (r >= u, r <= u)]


def _segment_runs(x):
    n_tiles = x.shape[0] // SUBLANES
    sub = lax.broadcasted_iota(jnp.int32, (SUBLANES, x.shape[1]), 0)
    pre_t, suf_t = [], []
    for i in range(n_tiles):
        p = s = x[i * SUBLANES:(i + 1) * SUBLANES]
        pre_i, suf_i = [], []
        n = 1
        while n < SUBLANES:
            pos = sub % (2 * n)
            add_p, add_s = jnp.zeros_like(p), jnp.zeros_like(s)
            for j in range(n):
                add_p = jnp.where(pos == n + j, pltpu.roll(p, j + 1, 0), add_p)
                add_s = jnp.where(pos == n - 1 - j, pltpu.roll(s, SUBLANES - (j + 1), 0), add_s)
            p, s = p + add_p, s + add_s
            pre_i.append(p)
            suf_i.append(s)
            n *= 2
        pre_t.append(pre_i)
        suf_t.append(suf_i)
    levels = len(pre_t[0])
    pre = [[t[m] for t in pre_t] for m in range(levels)]
    suf = [[t[m] for t in suf_t] for m in range(levels)]
    cur_p, cur_s = pre[-1], suf[-1]
    tot = [jnp.broadcast_to(t[SUBLANES - 1:SUBLANES, :], t.shape) for t in cur_p]
    seg = 1
    while seg < n_tiles:
        new_p, new_s, new_tot = list(cur_p), list(cur_s), list(tot)
        for base in range(0, n_tiles, 2 * seg):
            t_first, t_second = tot[base], tot[base + seg]
            both = t_first + t_second
            for i in range(base, base + seg):
                new_s[i] = cur_s[i] + t_second
                new_p[i + seg] = cur_p[i + seg] + t_first
                new_tot[i] = new_tot[i + seg] = both
        cur_p, cur_s, tot = new_p, new_s, new_tot
        pre.append(cur_p)
        suf.append(cur_s)
        seg *= 2
    join = lambda tiles: jnp.concatenate(tiles, axis=0)
    return [join(t) for t in pre], [join(t) for t in suf]


def _hgrn_kernel(q_ref, ff_ref, fb_ref, i_ref, g_ref, lb_ref, ng_ref, s0_ref, lvlf_ref, lvlb_ref, y_ref,
                 sfin_ref, of_scr, ob_scr, *, seq_len):
    blk, n_lvl = HGRN_BLOCK, HGRN_LEVELS
    n_blk = seq_len // blk

    def scan_blocks(chains):
        q, k, v, q_run, k_run = [], [], [], [], []
        for r0, f_ref, lb, _, forward, _, _ in chains:
            q.append(_silu(q_ref[pl.ds(r0, blk), :].astype(F32)))
            g = lb + (1.0 - lb) * _sigmoid(f_ref[pl.ds(r0, blk), :].astype(F32))
            k.append(1.0 - g)
            v.append(i_ref[pl.ds(r0, blk), :])
            log_g = jnp.log(g) * LOG2E
            pre, suf = _segment_runs(log_g)
            near, far = (pre, suf) if forward else (suf, pre)
            q_run.append([log_g] + near)
            k_run.append([None] + [x - log_g for x in far])
        kb = [x.astype(BF16) for x in k]
        level = [c[5][...] for c in chains]
        score = [jnp.where(lv == 0, lax.dot_general(qc.astype(BF16), kc, NT_DIMS, preferred_element_type=F32), 0.0)
                 for lv, qc, kc in zip(level, q, kb)]
        for m in range(n_lvl):
            for i in range(len(chains)):
                qs = (q[i] * jnp.exp2(q_run[i][m])).astype(BF16)
                ks = kb[i] if m == 0 else (k[i] * jnp.exp2(k_run[i][m])).astype(BF16)
                score[i] = jnp.where(level[i] == m + 1,
                                     lax.dot_general(qs, ks, NT_DIMS, preferred_element_type=F32), score[i])
        out = []
        for i, (_, _, _, st, _, _, edge) in enumerate(chains):
            if isinstance(st, int):
                st = out[st][1]
            q_all = jnp.exp2(q_run[i][n_lvl])
            o = jnp.dot(score[i].astype(BF16), v[i], preferred_element_type=F32)
            o += lax.dot_general((q[i] * q_all).astype(BF16), st.astype(BF16), NT_DIMS,
                                 preferred_element_type=F32)
            kd = (k[i] * jnp.exp2(k_run[i][n_lvl])).astype(BF16)
            st = st * q_all[edge:edge + 1, :] + lax.dot_general(v[i], kd, TN_DIMS, preferred_element_type=F32)
            out.append((o, st))
        return out

    lb_f, lb_b = lb_ref[0:1, :], lb_ref[1:2, :]

    def body(j, carry):
        st_f, st_b = carry
        r_f = pl.multiple_of(2 * j * blk, 2 * blk)
        r_b = pl.multiple_of((n_blk - 2 - 2 * j) * blk, 2 * blk)
        res = scan_blocks([(r_f, ff_ref, lb_f, st_f, True, lvlf_ref, blk - 1),
                           (r_b + blk, fb_ref, lb_b, st_b, False, lvlb_ref, 0),
                           (r_f + blk, ff_ref, lb_f, 0, True, lvlf_ref, blk - 1),
                           (r_b, fb_ref, lb_b, 1, False, lvlb_ref, 0)])
        of_scr[pl.ds(r_f, blk), :] = res[0][0]
        of_scr[pl.ds(r_f + blk, blk), :] = res[2][0]
        ob_scr[pl.ds(r_b + blk, blk), :] = res[1][0]
        ob_scr[pl.ds(r_b, blk), :] = res[3][0]
        return res[2][1], res[3][1]

    st_f, st_b = lax.fori_loop(0, n_blk // 2, body, (s0_ref[0].T, s0_ref[1].T))
    sfin_ref[0] = st_f.T
    sfin_ref[1] = st_b.T
    o = _rms(of_scr[...] + ob_scr[...]) * ng_ref[...]
    y_ref[...] = (o * _silu(g_ref[...].astype(F32))).astype(BF16)


def _hgrn(p_all, col, row0, n_seq, seq_len, n_heads, lb_l, norm_g, s0):
    assert row0 % seq_len == 0 and seq_len % (2 * HGRN_BLOCK) == 0
    seq0 = row0 // seq_len

    def cspec(name):
        return pl.BlockSpec((seq_len, LANE), lambda b, h: (seq0 + b, col[name] // LANE + h))

    st_spec = pl.BlockSpec((None, 2, None, LANE, LANE), lambda b, h: (b, 0, h, 0, 0))
    tables = _hgrn_level_codes()
    return pl.pallas_call(
        functools.partial(_hgrn_kernel, seq_len=seq_len),
        out_shape=[jax.ShapeDtypeStruct((n_seq * seq_len, n_heads * LANE), BF16),
                   jax.ShapeDtypeStruct(s0.shape, F32)],
        grid=(n_seq, n_heads),
        in_specs=[cspec("a_q"), cspec("a_ff"), cspec("a_fb"), cspec("a_i"), cspec("a_g"),
                  pl.BlockSpec((2, LANE), lambda b, h: (0, h)),
                  pl.BlockSpec((1, LANE), lambda b, h: (0, 0)),
                  st_spec] + [pl.BlockSpec(t.shape, lambda b, h: (0, 0)) for t in tables],
        out_specs=[pl.BlockSpec((seq_len, LANE), lambda b, h: (b, h)), st_spec],
        scratch_shapes=[pltpu.VMEM((seq_len, LANE), F32), pltpu.VMEM((seq_len, LANE), F32)],
        compiler_params=_params(("parallel", "parallel")),
        name="hgrn_%d" % seq_len,
    )(p_all, p_all, p_all, p_all, p_all, lb_l, norm_g.reshape(1, LANE), s0, *tables)


def _gmlp_kernel(u_ref, v_ref, gn_ref, ws_ref, bs_ref, o_ref, *, n_chunks, groups):
    u = _gelu_tanh(u_ref[...].astype(F32))
    v = (_rms(_gelu_tanh(v_ref[...].astype(F32))) * gn_ref[...]).astype(BF16)
    for n in range(n_chunks):
        rs = slice(n * MIX_CHUNK, (n + 1) * MIX_CHUNK)
        for g in range(groups):
            cs = slice(g * LANE, (g + 1) * LANE)
            mixed = jnp.dot(ws_ref[g], v[rs, cs], preferred_element_type=F32) + bs_ref[g]
            o_ref[rs, cs] = (u[rs, cs] * mixed).astype(BF16)


def _gmlp(p_all, col, norm_g, ws_bf, bias, tm):
    t = p_all.shape[0]
    groups = ws_bf.shape[0]
    cw = groups * LANE
    assert col["c_u"] % cw == 0 and col["c_v"] % cw == 0
    return pl.pallas_call(
        functools.partial(_gmlp_kernel, n_chunks=tm // MIX_CHUNK, groups=groups),
        out_shape=jax.ShapeDtypeStruct((t, cw), BF16),
        grid=(t // tm,),
        in_specs=[pl.BlockSpec((tm, cw), lambda i: (i, col["c_u"] // cw)),
                  pl.BlockSpec((tm, cw), lambda i: (i, col["c_v"] // cw)),
                  pl.BlockSpec((1, cw), lambda i: (0, 0)),
                  pl.BlockSpec((groups, MIX_CHUNK, MIX_CHUNK), lambda i: (0, 0, 0)),
                  pl.BlockSpec((groups, MIX_CHUNK, LANE), lambda i: (0, 0, 0))],
        out_specs=pl.BlockSpec((tm, cw), lambda i: (i, 0)),
        compiler_params=_params(("parallel",)),
        name="gmlp",
    )(p_all, p_all, norm_g.reshape(1, cw), ws_bf, bias)


def _route_rows(logit_rows):
    per = len(logit_rows) // N_GROUPS
    m = functools.reduce(jnp.maximum, logit_rows)
    p = [jnp.exp(r - m) for r in logit_rows]

    def top2_sum(a):
        pairs = [a[i] + a[j] for i in range(len(a)) for j in range(i + 1, len(a))]
        return functools.reduce(jnp.maximum, pairs)

    scores = [top2_sum(p[g * per:(g + 1) * per]) for g in range(N_GROUPS)]
    best, gi = scores[0], jnp.zeros(m.shape, jnp.int32)
    for g in range(1, N_GROUPS):
        better = scores[g] > best
        gi = jnp.where(better, g, gi)
        best = jnp.where(better, scores[g], best)
    a = []
    for j in range(per):
        aj = p[(N_GROUPS - 1) * per + j]
        for g in reversed(range(N_GROUPS - 1)):
            aj = jnp.where(gi == g, p[g * per + j], aj)
        a.append(aj)
    v1, i1 = a[0], jnp.zeros(m.shape, jnp.int32)
    for j in range(1, per):
        better = a[j] > v1
        i1 = jnp.where(better, j, i1)
        v1 = jnp.where(better, a[j], v1)
    v2, i2 = jnp.full(m.shape, -1.0, F32), jnp.zeros(m.shape, jnp.int32)
    for j in range(per):
        better = (i1 != j) & (a[j] > v2)
        i2 = jnp.where(better, j, i2)
        v2 = jnp.where(better, a[j], v2)
    inv = 1.0 / (v1 + v2)
    return gi * per + i1, gi * per + i2, v1 * inv, v2 * inv


def _merge_kernel(x_ref, ya_ctx_ref, ya_lat_ref, yb_ctx_ref, yb_lat_ref, yc_ref, ga0, ga1, gb0, gb1, gc0, gc1,
                  wpa_ref, wpb_ref, wpc_ref, wout_ref, g1_ref, n2_ref, sh2_ref, sc2_ref, rpair_ref, rb_ref,
                  x1_ref, h2_ref, rt_ref, cnt_ref, *, n_experts, ctx_tiles):
    def gate(r0, r1):
        return jnp.concatenate([_sigmoid(r0[...].astype(F32)), _sigmoid(r1[...].astype(F32))], axis=-1)

    is_ctx = pl.program_id(0) < ctx_tiles
    ya = jnp.where(is_ctx, ya_ctx_ref[...], ya_lat_ref[...])
    yb = jnp.where(is_ctx, yb_ctx_ref[...], yb_lat_ref[...])
    merged = gate(ga0, ga1) * jnp.dot(ya, wpa_ref[...], preferred_element_type=F32)
    merged += gate(gb0, gb1) * jnp.dot(yb, wpb_ref[...], preferred_element_type=F32)
    merged += gate(gc0, gc1) * jnp.dot(yc_ref[...], wpc_ref[...], preferred_element_type=F32)
    out = jnp.dot(merged.astype(BF16), wout_ref[...], preferred_element_type=F32)
    x1 = x_ref[...] + g1_ref[...] * out
    x1_ref[...] = x1
    h2 = _rms(x1) * n2_ref[...] * (1.0 + sc2_ref[...]) + sh2_ref[...]
    h2_ref[...] = h2.astype(BF16)
    hi, lo = _split2(h2)
    r_pair = rpair_ref[...]
    parts = (jnp.dot(hi, r_pair, preferred_element_type=F32) + jnp.dot(lo, r_pair, preferred_element_type=F32)).T
    lt = parts[0:n_experts, :] + parts[n_experts:2 * n_experts, :] + rb_ref[...]
    tm = lt.shape[1]
    e1, e2, w1, w2 = _route_rows([lt[e:e + 1, :] for e in range(n_experts)])

    @pl.when(pl.program_id(0) == 0)
    def _():
        cnt_ref[...] = jnp.zeros_like(cnt_ref)

    e_id = lax.broadcasted_iota(jnp.int32, (n_experts, tm), 0)
    hit1, hit2 = e_id == e1, e_id == e2
    onehot = (hit1 | hit2).astype(BF16)
    src = lax.broadcasted_iota(jnp.int32, (tm, tm), 0)
    dst = lax.broadcasted_iota(jnp.int32, (tm, tm), 1)
    before = jnp.dot(onehot, (src < dst).astype(BF16), preferred_element_type=F32) + cnt_ref[...]
    rank1 = jnp.sum(jnp.where(hit1, before, 0.0), axis=0, keepdims=True)
    rank2 = jnp.sum(jnp.where(hit2, before, 0.0), axis=0, keepdims=True)
    cnt_ref[...] += jnp.dot(onehot, jnp.ones((tm, tm), BF16), preferred_element_type=F32)
    zero = jnp.zeros_like(w1)
    rt_ref[...] = jnp.concatenate([e1.astype(F32), e2.astype(F32), w1, w2, rank1, rank2, zero, zero], axis=0)


ROUTE_ROWS = 8


def _merge(rows, x, y_a, y_b, y_c, p_all, col, mod_l, norm2_g, w_pa, w_pb, w_pc, w_out, layer, r_pair,
           r_bias, tm):
    t, d = x.shape
    n_experts = r_bias.shape[0]
    gw = d // 2
    assert col["gates"] % gw == 0
    g0 = col["gates"] // gw
    ctx_tiles = rows.n_ctx // tm
    row = lambda w: pl.BlockSpec((tm, w), lambda i: (i, 0))
    ctx_row = lambda w: pl.BlockSpec((tm, w), lambda i: (jnp.minimum(i, ctx_tiles - 1), 0))
    lat_row = lambda w: pl.BlockSpec((tm, w), lambda i: (jnp.maximum(i - ctx_tiles, 0), 0))
    full = lambda a: pl.BlockSpec(a.shape, lambda i: (0,) * a.ndim, pipeline_mode=pl.Buffered(1))
    of_layer = lambda a: pl.BlockSpec((None,) + a.shape[1:], lambda i: (layer,) + (0,) * (a.ndim - 1),
                                      pipeline_mode=pl.Buffered(1))
    gate_specs = [pl.BlockSpec((tm, gw), functools.partial(lambda i, k: (i, g0 + k), k=k)) for k in range(6)]
    return pl.pallas_call(
        functools.partial(_merge_kernel, n_experts=n_experts, ctx_tiles=ctx_tiles),
        out_shape=[jax.ShapeDtypeStruct((t, d), F32), jax.ShapeDtypeStruct((t, d), BF16),
                   jax.ShapeDtypeStruct((ROUTE_ROWS, t), F32), jax.ShapeDtypeStruct((n_experts, tm), F32)],
        grid=(t // tm,),
        in_specs=[row(d), ctx_row(y_a[0].shape[1]), lat_row(y_a[1].shape[1]), ctx_row(y_b[0].shape[1]),
                  lat_row(y_b[1].shape[1]), row(y_c.shape[1])] + gate_specs
                 + [of_layer(w_pa), of_layer(w_pb), of_layer(w_pc), of_layer(w_out), rows.mod_spec(2, tm, d),
                    pl.BlockSpec((1, d), lambda i: (0, 0)), rows.mod_spec(3, tm, d), rows.mod_spec(4, tm, d),
                    full(r_pair), pl.BlockSpec((n_experts, tm), lambda i: (0, 0))],
        out_specs=[row(d), row(d), pl.BlockSpec((ROUTE_ROWS, tm), lambda i: (0, i)),
                   pl.BlockSpec((n_experts, tm), lambda i: (0, 0))],
        compiler_params=_params(("arbitrary",)),
        name="merge_out_proj",
    )(x, *y_a, *y_b, y_c, *([p_all] * 6), w_pa, w_pb, w_pc, w_out, mod_l, norm2_g.reshape(1, d), mod_l, mod_l,
      r_pair, jnp.broadcast_to(r_bias.astype(F32)[:, None], (n_experts, tm)))


def _moe_kernel(be_ref, bv_ref, x_ref, wg_ref, wu_ref, wd_ref, o_ref):
    i = pl.program_id(0)

    @pl.when(bv_ref[i] > 0)
    def _():
        x = x_ref[...]
        a = jnp.dot(x, wg_ref[...], preferred_element_type=F32)
        b = jnp.dot(x, wu_ref[...], preferred_element_type=F32)
        h = (_silu(a) * b).astype(BF16)
        o_ref[...] = jnp.dot(h, wd_ref[...], preferred_element_type=F32).astype(BF16)

    @pl.when(bv_ref[i] == 0)
    def _():
        o_ref[...] = jnp.zeros_like(o_ref)


def _moe_ffn(xb, block_e, block_valid, wg, wu, wd, layer, tm):
    n_slots, d = xb.shape
    ff = wg.shape[3]
    grid_spec = pltpu.PrefetchScalarGridSpec(
        num_scalar_prefetch=2,
        grid=(n_slots // tm,),
        in_specs=[pl.BlockSpec((tm, d), lambda i, be, bv: (i, 0)),
                  pl.BlockSpec((None, None, d, ff), lambda i, be, bv: (layer, be[i], 0, 0)),
                  pl.BlockSpec((None, None, d, ff), lambda i, be, bv: (layer, be[i], 0, 0)),
                  pl.BlockSpec((None, None, ff, d), lambda i, be, bv: (layer, be[i], 0, 0))],
        out_specs=pl.BlockSpec((tm, d), lambda i, be, bv: (i, 0)),
    )
    return pl.pallas_call(
        _moe_kernel,
        out_shape=jax.ShapeDtypeStruct((n_slots, d), BF16),
        grid_spec=grid_spec,
        compiler_params=_params(("arbitrary",)),
        name="moe_experts",
    )(block_e, block_valid, xb, wg, wu, wd)


def _dispatch(expert, rank, counts, tm):
    n_experts = counts.shape[0]
    n_tok = expert.shape[1]
    n_assign = TOP_K * n_tok
    padded = (counts + tm - 1) // tm * tm
    pad_end = jnp.cumsum(padded)
    pad_start = pad_end - padded
    first_slot = jnp.sum(jnp.where(expert[..., None] == jnp.arange(n_experts), pad_start, 0), axis=-1)
    dest = first_slot + rank
    n_blocks = -(-(n_assign + n_experts * (tm - 1)) // tm)
    tok = jnp.broadcast_to(jnp.arange(n_tok, dtype=jnp.int32)[None, :], dest.shape)
    slot_tok = jnp.zeros((n_blocks * tm,), jnp.int32).at[dest.reshape(-1)].set(
        tok.reshape(-1), unique_indices=True, mode="promise_in_bounds")
    block_start = jnp.arange(n_blocks, dtype=jnp.int32) * tm
    block_e = jnp.minimum(jnp.sum(block_start[:, None] >= pad_end[None, :], axis=1), n_experts - 1)
    block_valid = (block_start < pad_end[-1]).astype(jnp.int32)
    return slot_tok, dest, block_e.astype(jnp.int32), block_valid


def _final_kernel(x_ref, moe_ref, g2_ref, nf_ref, o_ref):
    x = x_ref[...] + g2_ref[...] * moe_ref[...].astype(F32)
    o_ref[...] = _rms(x) * nf_ref[...]


def _final_norm(rows, x, moe, mod_l, norm_f, row0, n_rows, tm):
    d = x.shape[1]
    t0 = row0 // tm
    row_spec = pl.BlockSpec((tm, d), lambda i: (t0 + i, 0))
    g2_spec = pl.BlockSpec((None, 1, d), lambda i: (rows.cond_row(t0 + i, tm) * N_MOD + 5, 0, 0))
    return pl.pallas_call(
        _final_kernel,
        out_shape=jax.ShapeDtypeStruct((n_rows, d), F32),
        grid=(n_rows // tm,),
        in_specs=[row_spec, row_spec, g2_spec, pl.BlockSpec((1, d), lambda i: (0, 0))],
        out_specs=pl.BlockSpec((tm, d), lambda i: (i, 0)),
        compiler_params=_params(("parallel",)),
        name="final_norm",
    )(x, moe, mod_l, norm_f.reshape(1, d))


def kernel(x_prompt, x_sample, cache_k, cache_v, state_hgrn, c, c_ctx, w_mod, b_mod, norm1, norm2, norm_f,
           w_in, hgrn_lb, hgrn_norm, q_norm, k_norm, gmlp_norm, gmlp_ws, gmlp_bs, w_pa, w_pb, w_pc, w_out,
           router_w, router_b, exp_gate, exp_up, exp_down):
    n_ctx_seq, ctx_len, d = x_prompt.shape
    n_lat_seq, lat_len, _ = x_sample.shape
    depth = w_mod.shape[0]
    past, kvh = cache_k.shape[2], cache_k.shape[3]
    a_heads = state_hgrn.shape[3]
    a_width, b_width, c_width = hgrn_lb.shape[2], w_pb.shape[1], w_pc.shape[1]
    b_heads = b_width // LANE
    n_experts = router_w.shape[1]
    assert cache_k.shape[4] == LANE and state_hgrn.shape[4:] == (LANE, LANE) and a_width == a_heads * LANE

    sizes = [("a_q", a_width), ("a_ff", a_width), ("a_fb", a_width), ("a_i", a_width), ("a_g", a_width),
             ("b_q", b_width), ("b_k", kvh * LANE), ("b_v", kvh * LANE), ("c_u", c_width), ("c_v", c_width),
             ("gates", 3 * d)]
    col, acc = {}, 0
    for name, width in sizes:
        col[name] = acc
        acc += width
    assert acc == w_in.shape[2]

    n_ctx = n_ctx_seq * ctx_len
    rows = _Rows(n_ctx, n_lat_seq, lat_len)
    tm_proj = min(1024, n_ctx, lat_len)
    tn_proj = min(1024, math.gcd(w_in.shape[2], 1024))
    tm_row = min(256, n_ctx, lat_len)
    tm_moe = 256
    tq_lat = min(256, lat_len)

    cond = jnp.concatenate([c_ctx[None, :], c, jnp.zeros((COND_ROWS - 1 - n_lat_seq, d), F32)], axis=0)
    mod = _modulation(cond, w_mod, b_mod).reshape(depth, COND_ROWS * N_MOD, 1, d)

    p_lb = jax.nn.softmax(hgrn_lb.astype(F32), axis=1)
    lbs = jnp.cumsum(p_lb, axis=1) - p_lb[:, :1]

    cos, sin, swap = _rope_tables(lat_len)
    cache_k2 = cache_k.reshape(n_lat_seq, depth, past, kvh * LANE)
    cache_v2 = cache_v.reshape(n_lat_seq, depth, past, kvh * LANE)
    s0_ctx = jnp.zeros((n_ctx_seq,) + state_hgrn.shape[2:], F32)
    assert 2 * n_experts <= LANE and n_experts % SUBLANES == 0
    r_hi, r_lo = _split2(router_w.astype(F32))
    r_pair = jnp.zeros((d, LANE), BF16).at[:, :n_experts].set(r_hi).at[:, n_experts:2 * n_experts].set(r_lo)
    gmlp_bias = jnp.broadcast_to(gmlp_bs[:, :, :, None], gmlp_bs.shape + (LANE,))

    w_in_bf, w_pa_bf, w_pb_bf, w_pc_bf, w_out_bf, gate_bf, up_bf, down_bf = (
        w.astype(BF16) for w in (w_in, w_pa, w_pb, w_pc, w_out, exp_gate, exp_up, exp_down))

    x = jnp.concatenate([x_prompt.reshape(n_ctx, d), x_sample.reshape(n_lat_seq * lat_len, d)], axis=0)
    moe = None
    new_k, new_v, new_s = [], [], []
    for l in range(depth):
        mod_l = mod[l]
        p_all, x = _in_proj(rows, x, moe, mod[l - 1] if l else None, mod_l, norm1[l], w_in_bf, l,
                            tm_proj, tn_proj)
        yb_ctx, k_l, v_l = _attention(p_all, col, 0, n_ctx_seq, ctx_len, b_heads, kvh, q_norm[l], k_norm[l],
                                      ctx_len)
        (yb_lat,) = _attention(p_all, col, n_ctx, n_lat_seq, lat_len, b_heads, kvh, q_norm[l], k_norm[l],
                               tq_lat, ctx=(cache_k2, cache_v2, l, cos, sin, swap))
        ya_ctx, s_l = _hgrn(p_all, col, 0, n_ctx_seq, ctx_len, a_heads, lbs[:, l], hgrn_norm[l], s0_ctx)
        ya_lat, _ = _hgrn(p_all, col, n_ctx, n_lat_seq, lat_len, a_heads, lbs[:, l], hgrn_norm[l],
                          state_hgrn[:, l])
        y_c = _gmlp(p_all, col, gmlp_norm[l], gmlp_ws[l].astype(BF16), gmlp_bias[l], tm_proj)
        x, h2, route, counts = _merge(rows, x, (ya_ctx, ya_lat), (yb_ctx, yb_lat), y_c, p_all, col, mod_l,
                                      norm2[l], w_pa_bf, w_pb_bf, w_pc_bf, w_out_bf, l, r_pair, router_b,
                                      tm_row)
        slot_tok, dest, block_e, block_valid = _dispatch(
            route[0:2].astype(jnp.int32), route[4:6].astype(jnp.int32), counts[:, 0].astype(jnp.int32), tm_moe)
        rows_of = lambda a, idx: a.at[idx].get(mode="promise_in_bounds")
        yb = _moe_ffn(rows_of(h2, slot_tok), block_e, block_valid, gate_bf, up_bf, down_bf, l, tm_moe)
        moe = (rows_of(yb, dest[0]).astype(F32) * route[2][:, None]
               + rows_of(yb, dest[1]).astype(F32) * route[3][:, None]).astype(BF16)
        new_k.append(k_l)
        new_v.append(v_l)
        new_s.append(s_l)

    y_ctx = _final_norm(rows, x, moe, mod[depth - 1], norm_f, 0, n_ctx, tm_row)
    y_lat = _final_norm(rows, x, moe, mod[depth - 1], norm_f, n_ctx, n_lat_seq * lat_len, tm_row)
    kv_shape = (n_ctx_seq, depth, ctx_len, kvh, LANE)
    return (y_ctx.reshape(n_ctx_seq, ctx_len, d), y_lat.reshape(n_lat_seq, lat_len, d),
            jnp.stack(new_k, axis=1).reshape(kv_shape), jnp.stack(new_v, axis=1).reshape(kv_shape),
            jnp.stack(new_s, axis=1))
```

```python
import functools
import math

import jax
import jax.numpy as jnp
import numpy as np
from jax import lax
from jax.experimental import pallas as pl
from jax.experimental.pallas import tpu as pltpu

F32 = jnp.float32
BF16 = jnp.bfloat16

EPS = 1e-6
N_MOD = 6
GRID_W = 64
ROPE_THETA = 10000.0
MIX_CHUNK = 128
N_GROUPS = 4
TOP_K = 2
LANE = 128
SUBLANES = 8
HGRN_BLOCK = 128
HGRN_LEVELS = 7
ATTN_KEY_CHUNK = 1024
LOG2E = 1.4426950408889634
COND_ROWS = 16
VMEM_LIMIT = 56 * 1024 * 1024

NT_DIMS = (((1,), (1,)), ((), ()))
TN_DIMS = (((0,), (0,)), ((), ()))


def _sigmoid(x):
    return 1.0 / (1.0 + jnp.exp(-x))


def _silu(x):
    return x * _sigmoid(x)


def _gelu_tanh(x):
    return 0.5 * x * (1.0 + jnp.tanh(math.sqrt(2.0 / math.pi) * (x + 0.044715 * (x * x * x))))


def _rms(x):
    return x * lax.rsqrt(jnp.mean(x * x, axis=-1, keepdims=True) + EPS)


def _split2(x):
    hi = x.astype(BF16)
    lo = (x - hi.astype(F32)).astype(BF16)
    return hi, lo


def _split3(x):
    hi = x.astype(BF16)
    r = x - hi.astype(F32)
    mid = r.astype(BF16)
    lo = (r - mid.astype(F32)).astype(BF16)
    return hi, mid, lo


def _params(sem):
    return pltpu.CompilerParams(dimension_semantics=sem, vmem_limit_bytes=VMEM_LIMIT)


def _mod_kernel(c_ref, w_ref, b_ref, o_ref):
    s = _silu(c_ref[...]).astype(BF16)
    o_ref[...] = jnp.dot(s, w_ref[...].astype(BF16), preferred_element_type=F32) + b_ref[...]


def _modulation(cond, w_mod, b_mod):
    depth, d, n = w_mod.shape
    tn = math.gcd(n, 1024)
    return pl.pallas_call(
        _mod_kernel,
        out_shape=jax.ShapeDtypeStruct((depth, COND_ROWS, n), F32),
        grid=(depth, n // tn),
        in_specs=[
            pl.BlockSpec((COND_ROWS, d), lambda l, j: (0, 0)),
            pl.BlockSpec((None, d, tn), lambda l, j: (l, 0, j)),
            pl.BlockSpec((None, 1, tn), lambda l, j: (l, 0, j)),
        ],
        out_specs=pl.BlockSpec((None, COND_ROWS, tn), lambda l, j: (l, 0, j)),
        compiler_params=_params(("parallel", "parallel")),
        name="adaln_modulation",
    )(cond, w_mod, b_mod.reshape(depth, 1, n))


class _Rows:
    def __init__(self, n_ctx, n_lat_seq, lat_len):
        self.n_ctx, self.lat_len = n_ctx, lat_len
        self.total = n_ctx + n_lat_seq * lat_len

    def cond_row(self, i, tm):
        ctx_tiles = self.n_ctx // tm
        per_seq = self.lat_len // tm
        return jnp.where(i < ctx_tiles, 0, 1 + (i - ctx_tiles) // per_seq)

    def mod_spec(self, k, tm, d):
        return pl.BlockSpec((None, 1, d), lambda i, *_: (self.cond_row(i, tm) * N_MOD + k, 0, 0))


def _in_proj_kernel(*refs, combine):
    if combine:
        x_ref, moe_ref, g2_ref, n_ref, sh_ref, sc_ref, w_ref, o_ref, x2_ref, h_scr = refs
    else:
        x_ref, n_ref, sh_ref, sc_ref, w_ref, o_ref, h_scr = refs

    @pl.when(pl.program_id(1) == 0)
    def _():
        x = x_ref[...]
        if combine:
            x = x + g2_ref[...] * moe_ref[...].astype(F32)
            x2_ref[...] = x
        h = _rms(x) * n_ref[...] * (1.0 + sc_ref[...]) + sh_ref[...]
        h_scr[...] = h.astype(BF16)

    o_ref[...] = jnp.dot(h_scr[...], w_ref[...], preferred_element_type=F32).astype(BF16)


def _in_proj(rows, x, moe, mod_prev, mod_l, norm_g, w_bf, layer, tm, tn):
    t, d = x.shape
    n = w_bf.shape[2]
    combine = moe is not None
    row_spec = pl.BlockSpec((tm, d), lambda i, j: (i, 0))
    row_once_spec = pl.BlockSpec((tm, d), lambda i, j: (i, 0), pipeline_mode=pl.Buffered(1))
    vec_spec = pl.BlockSpec((1, d), lambda i, j: (0, 0))
    in_specs = [row_once_spec]
    args = [x]
    if combine:
        in_specs += [row_once_spec, rows.mod_spec(5, tm, d)]
        args += [moe, mod_prev]
    in_specs += [vec_spec, rows.mod_spec(0, tm, d), rows.mod_spec(1, tm, d),
                 pl.BlockSpec((None, d, tn), lambda i, j: (layer, 0, j))]
    args += [norm_g.reshape(1, d), mod_l, mod_l, w_bf]
    out_shape = [jax.ShapeDtypeStruct((t, n), BF16)]
    out_specs = [pl.BlockSpec((tm, tn), lambda i, j: (i, j))]
    if combine:
        out_shape.append(jax.ShapeDtypeStruct((t, d), F32))
        out_specs.append(row_spec)
    res = pl.pallas_call(
        functools.partial(_in_proj_kernel, combine=combine),
        out_shape=out_shape,
        grid=(t // tm, n // tn),
        in_specs=in_specs,
        out_specs=out_specs,
        scratch_shapes=[pltpu.VMEM((tm, d), BF16)],
        compiler_params=_params(("parallel", "arbitrary")),
        name="in_proj",
    )(*args)
    return (res[0], res[1]) if combine else (res[0], x)


def _rope(x, cos, sin_signed, swap):
    swapped = jnp.dot(x.astype(BF16), swap, preferred_element_type=F32)
    return x * cos + swapped * sin_signed


def _attn_kernel(*refs, latent, past, tq, grp, scale):
    if latent:
        (q_ref, kn_ref, vn_ref, ck_ref, cv_ref, cos_ref, sin_ref, swap_ref, qg_ref, kg_ref,
         o_ref, k_scr, v_scr) = refs
    else:
        q_ref, kn_ref, vn_ref, qg_ref, kg_ref, o_ref, ko_ref, vo_ref, k_scr, v_scr = refs
    qi = pl.program_id(2)

    @pl.when(qi == 0)
    def _():
        k = _rms(kn_ref[...].astype(F32)) * kg_ref[...]
        if latent:
            k = _rope(k, cos_ref[...], sin_ref[...], swap_ref[...])
            k_scr[0:past, :] = ck_ref[...].astype(BF16)
            v_scr[0:past, 0:LANE] = cv_ref[...].astype(BF16)
        else:
            ko_ref[...] = k
            vo_ref[...] = vn_ref[...].astype(F32)
        k_scr[past:, :] = k.astype(BF16)
        v_scr[past:, 0:LANE] = vn_ref[...]
        v_scr[:, LANE:] = jnp.ones((v_scr.shape[0], LANE), BF16)

    q = q_ref[...].astype(F32)
    heads = []
    for h in range(grp):
        qh = _rms(q[:, h * LANE:(h + 1) * LANE]) * qg_ref[...]
        if latent:
            r0 = pl.multiple_of(qi * tq, tq)
            qh = _rope(qh, cos_ref[pl.ds(r0, tq), :], sin_ref[pl.ds(r0, tq), :], swap_ref[...])
        heads.append((qh * scale).astype(BF16))
    qs = jnp.concatenate(heads, axis=0)
    n_keys = k_scr.shape[0]
    chunks = [(c0, min(c0 + ATTN_KEY_CHUNK, n_keys)) for c0 in range(0, n_keys, ATTN_KEY_CHUNK)]

    def scores(c):
        return lax.dot_general(qs, k_scr[c[0]:c[1], :], NT_DIMS, preferred_element_type=F32)

    m = acc = None
    s_next = scores(chunks[0])
    for n, (c0, c1) in enumerate(chunks):
        s = s_next
        if n + 1 < len(chunks):
            s_next = scores(chunks[n + 1])
        m_c = jnp.max(functools.reduce(jnp.maximum, [s[:, t:t + LANE] for t in range(0, c1 - c0, LANE)]),
                      axis=-1, keepdims=True)
        m_new = m_c if m is None else jnp.maximum(m, m_c)
        pv = jnp.dot(jnp.exp2(s - m_new).astype(BF16), v_scr[c0:c1, :], preferred_element_type=F32)
        acc = pv if m is None else acc * jnp.exp2(m - m_new) + pv
        m = m_new
    o = acc[:, 0:LANE] / acc[:, LANE:]
    for h in range(grp):
        o_ref[:, h * LANE:(h + 1) * LANE] = o[h * tq:(h + 1) * tq, :].astype(BF16)


def _attention(p_all, col, row0, n_seq, seq_len, n_heads, kvh, q_gain, k_gain, tq, ctx=None):
    grp = n_heads // kvh
    latent = ctx is not None
    qw = grp * LANE
    assert col["b_q"] % qw == 0 and row0 % seq_len == 0 and seq_len % tq == 0
    seq0, nq = row0 // seq_len, seq_len // tq
    q_spec = pl.BlockSpec((tq, qw), lambda b, g, i: (seq0 * nq + b * nq + i, col["b_q"] // qw + g))
    k_spec = pl.BlockSpec((seq_len, LANE), lambda b, g, i: (seq0 + b, col["b_k"] // LANE + g))
    v_spec = pl.BlockSpec((seq_len, LANE), lambda b, g, i: (seq0 + b, col["b_v"] // LANE + g))
    gain_spec = pl.BlockSpec((1, LANE), lambda b, g, i: (0, 0))
    in_specs = [q_spec, k_spec, v_spec]
    args = [p_all, p_all, p_all]
    past = 0
    if latent:
        cache_k, cache_v, layer, cos, sin, swap = ctx
        past = cache_k.shape[2]
        c_spec = pl.BlockSpec((None, None, past, LANE), lambda b, g, i: (b, layer, 0, g))
        t_spec = pl.BlockSpec((seq_len, LANE), lambda b, g, i: (0, 0))
        in_specs += [c_spec, c_spec, t_spec, t_spec, pl.BlockSpec((LANE, LANE), lambda b, g, i: (0, 0))]
        args += [cache_k, cache_v, cos, sin, swap]
    in_specs += [gain_spec, gain_spec]
    args += [q_gain.reshape(1, LANE), k_gain.reshape(1, LANE)]
    out_shape = [jax.ShapeDtypeStruct((n_seq * seq_len, n_heads * LANE), BF16)]
    out_specs = [pl.BlockSpec((tq, qw), lambda b, g, i: (b * nq + i, g))]
    if not latent:
        kv_shape = jax.ShapeDtypeStruct((n_seq, seq_len, kvh * LANE), F32)
        kv_spec = pl.BlockSpec((None, seq_len, LANE), lambda b, g, i: (b, 0, g))
        out_shape += [kv_shape, kv_shape]
        out_specs += [kv_spec, kv_spec]
    return pl.pallas_call(
        functools.partial(_attn_kernel, latent=latent, past=past, tq=tq, grp=grp, scale=LANE ** -0.5 * LOG2E),
        out_shape=out_shape,
        grid=(n_seq, kvh, nq),
        in_specs=in_specs,
        out_specs=out_specs,
        scratch_shapes=[pltpu.VMEM((past + seq_len, LANE), BF16), pltpu.VMEM((past + seq_len, 2 * LANE), BF16)],
        compiler_params=_params(("parallel", "parallel", "arbitrary")),
        name="attention_latent" if latent else "attention_context",
    )(*args)


def _rope_tables(seq_len):
    quarter = LANE // 4
    inv = ROPE_THETA ** (-jnp.arange(quarter, dtype=F32) / quarter)
    pos = jnp.arange(seq_len)
    row = (pos // GRID_W).astype(F32)[:, None] * inv[None, :]
    colp = (pos % GRID_W).astype(F32)[:, None] * inv[None, :]
    cos = jnp.concatenate([jnp.cos(row), jnp.cos(row), jnp.cos(colp), jnp.cos(colp)], axis=-1)
    sin = jnp.concatenate([-jnp.sin(row), jnp.sin(row), -jnp.sin(colp), jnp.sin(colp)], axis=-1)
    lane = np.arange(LANE)
    partner = np.where((lane // quarter) % 2 == 0, lane + quarter, lane - quarter)
    swap = jnp.asarray(lane[:, None] == partner[None, :], BF16)
    return cos, sin, swap


def _hgrn_level_codes():
    r = np.arange(HGRN_BLOCK)[:, None]
    u = np.arange(HGRN_BLOCK)[None, :]
    diff = r ^ u
    code = np.where(diff == 0, 0, 1 + np.floor(np.log2(np.maximum(diff, 1))).astype(np.int32))
    return [jnp.asarray(np.where(keep, code, -1), jnp.int32) for keep in (r >= u, r <= u)]


def _segment_runs(x):
    n_tiles = x.shape[0] // SUBLANES
    sub = lax.broadcasted_iota(jnp.int32, (SUBLANES, x.shape[1]), 0)
    pre_t, suf_t = [], []
    for i in range(n_tiles):
        p = s = x[i * SUBLANES:(i + 1) * SUBLANES]
        pre_i, suf_i = [], []
        n = 1
        while n < SUBLANES:
            pos = sub % (2 * n)
            add_p, add_s = jnp.zeros_like(p), jnp.zeros_like(s)
            for j in range(n):
                add_p = jnp.where(pos == n + j, pltpu.roll(p, j + 1, 0), add_p)
                add_s = jnp.where(pos == n - 1 - j, pltpu.roll(s, SUBLANES - (j + 1), 0), add_s)
            p, s = p + add_p, s + add_s
            pre_i.append(p)
            suf_i.append(s)
            n *= 2
        pre_t.append(pre_i)
        suf_t.append(suf_i)
    levels = len(pre_t[0])
    pre = [[t[m] for t in pre_t] for m in range(levels)]
    suf = [[t[m] for t in suf_t] for m in range(levels)]
    cur_p, cur_s = pre[-1], suf[-1]
    tot = [jnp.broadcast_to(t[SUBLANES - 1:SUBLANES, :], t.shape) for t in cur_p]
    seg = 1
    while seg < n_tiles:
        new_p, new_s, new_tot = list(cur_p), list(cur_s), list(tot)
        for base in range(0, n_tiles, 2 * seg):
            t_first, t_second = tot[base], tot[base + seg]
            both = t_first + t_second
            for i in range(base, base + seg):
                new_s[i] = cur_s[i] + t_second
                new_p[i + seg] = cur_p[i + seg] + t_first
                new_tot[i] = new_tot[i + seg] = both
        cur_p, cur_s, tot = new_p, new_s, new_tot
        pre.append(cur_p)
        suf.append(cur_s)
        seg *= 2
    join = lambda tiles: jnp.concatenate(tiles, axis=0)
    return [join(t) for t in pre], [join(t) for t in suf]


def _hgrn_kernel(q_ref, ff_ref, fb_ref, i_ref, g_ref, lb_ref, ng_ref, s0_ref, lvlf_ref, lvlb_ref, y_ref,
                 sfin_ref, of_scr, ob_scr, *, seq_len):
    blk, n_lvl = HGRN_BLOCK, HGRN_LEVELS
    n_blk = seq_len // blk

    def scan_blocks(chains):
        q, k, v, q_run, k_run = [], [], [], [], []
        for r0, f_ref, lb, _, forward, _, _ in chains:
            q.append(_silu(q_ref[pl.ds(r0, blk), :].astype(F32)))
            g = lb + (1.0 - lb) * _sigmoid(f_ref[pl.ds(r0, blk), :].astype(F32))
            k.append(1.0 - g)
            v.append(i_ref[pl.ds(r0, blk), :])
            log_g = jnp.log(g) * LOG2E
            pre, suf = _segment_runs(log_g)
            near, far = (pre, suf) if forward else (suf, pre)
            q_run.append([log_g] + near)
            k_run.append([None] + [x - log_g for x in far])
        kb = [x.astype(BF16) for x in k]
        level = [c[5][...] for c in chains]
        score = [jnp.where(lv == 0, lax.dot_general(qc.astype(BF16), kc, NT_DIMS, preferred_element_type=F32), 0.0)
                 for lv, qc, kc in zip(level, q, kb)]
        for m in range(n_lvl):
            for i in range(len(chains)):
                qs = (q[i] * jnp.exp2(q_run[i][m])).astype(BF16)
                ks = kb[i] if m == 0 else (k[i] * jnp.exp2(k_run[i][m])).astype(BF16)
                score[i] = jnp.where(level[i] == m + 1,
                                     lax.dot_general(qs, ks, NT_DIMS, preferred_element_type=F32), score[i])
        out = []
        for i, (_, _, _, st, _, _, edge) in enumerate(chains):
            if isinstance(st, int):
                st = out[st][1]
            q_all = jnp.exp2(q_run[i][n_lvl])
            o = jnp.dot(score[i].astype(BF16), v[i], preferred_element_type=F32)
            o += lax.dot_general((q[i] * q_all).astype(BF16), st.astype(BF16), NT_DIMS,
                                 preferred_element_type=F32)
            kd = (k[i] * jnp.exp2(k_run[i][n_lvl])).astype(BF16)
            st = st * q_all[edge:edge + 1, :] + lax.dot_general(v[i], kd, TN_DIMS, preferred_element_type=F32)
            out.append((o, st))
        return out

    lb_f, lb_b = lb_ref[0:1, :], lb_ref[1:2, :]

    def body(j, carry):
        st_f, st_b = carry
        r_f = pl.multiple_of(2 * j * blk, 2 * blk)
        r_b = pl.multiple_of((n_blk - 2 - 2 * j) * blk, 2 * blk)
        res = scan_blocks([(r_f, ff_ref, lb_f, st_f, True, lvlf_ref, blk - 1),
                           (r_b + blk, fb_ref, lb_b, st_b, False, lvlb_ref, 0),
                           (r_f + blk, ff_ref, lb_f, 0, True, lvlf_ref, blk - 1),
                           (r_b, fb_ref, lb_b, 1, False, lvlb_ref, 0)])
        of_scr[pl.ds(r_f, blk), :] = res[0][0]
        of_scr[pl.ds(r_f + blk, blk), :] = res[2][0]
        ob_scr[pl.ds(r_b + blk, blk), :] = res[1][0]
        ob_scr[pl.ds(r_b, blk), :] = res[3][0]
        return res[2][1], res[3][1]

    st_f, st_b = lax.fori_loop(0, n_blk // 2, body, (s0_ref[0].T, s0_ref[1].T))
    sfin_ref[0] = st_f.T
    sfin_ref[1] = st_b.T
    o = _rms(of_scr[...] + ob_scr[...]) * ng_ref[...]
    y_ref[...] = (o * _silu(g_ref[...].astype(F32))).astype(BF16)


def _hgrn(p_all, col, row0, n_seq, seq_len, n_heads, lb_l, norm_g, s0):
    assert row0 % seq_len == 0 and seq_len % (2 * HGRN_BLOCK) == 0
    seq0 = row0 // seq_len

    def cspec(name):
        return pl.BlockSpec((seq_len, LANE), lambda b, h: (seq0 + b, col[name] // LANE + h))

    st_spec = pl.BlockSpec((None, 2, None, LANE, LANE), lambda b, h: (b, 0, h, 0, 0))
    tables = _hgrn_level_codes()
    return pl.pallas_call(
        functools.partial(_hgrn_kernel, seq_len=seq_len),
        out_shape=[jax.ShapeDtypeStruct((n_seq * seq_len, n_heads * LANE), BF16),
                   jax.ShapeDtypeStruct(s0.shape, F32)],
        grid=(n_seq, n_heads),
        in_specs=[cspec("a_q"), cspec("a_ff"), cspec("a_fb"), cspec("a_i"), cspec("a_g"),
                  pl.BlockSpec((2, LANE), lambda b, h: (0, h)),
                  pl.BlockSpec((1, LANE), lambda b, h: (0, 0)),
                  st_spec] + [pl.BlockSpec(t.shape, lambda b, h: (0, 0)) for t in tables],
        out_specs=[pl.BlockSpec((seq_len, LANE), lambda b, h: (b, h)), st_spec],
        scratch_shapes=[pltpu.VMEM((seq_len, LANE), F32), pltpu.VMEM((seq_len, LANE), F32)],
        compiler_params=_params(("parallel", "parallel")),
        name="hgrn_%d" % seq_len,
    )(p_all, p_all, p_all, p_all, p_all, lb_l, norm_g.reshape(1, LANE), s0, *tables)


def _gmlp_kernel(u_ref, v_ref, gn_ref, ws_ref, bs_ref, o_ref, *, n_chunks, groups):
    u = _gelu_tanh(u_ref[...].astype(F32))
    v = (_rms(_gelu_tanh(v_ref[...].astype(F32))) * gn_ref[...]).astype(BF16)
    for n in range(n_chunks):
        rs = slice(n * MIX_CHUNK, (n + 1) * MIX_CHUNK)
        for g in range(groups):
            cs = slice(g * LANE, (g + 1) * LANE)
            mixed = jnp.dot(ws_ref[g], v[rs, cs], preferred_element_type=F32) + bs_ref[g]
            o_ref[rs, cs] = (u[rs, cs] * mixed).astype(BF16)


def _gmlp(p_all, col, norm_g, ws_bf, bias, tm):
    t = p_all.shape[0]
    groups = ws_bf.shape[0]
    cw = groups * LANE
    assert col["c_u"] % cw == 0 and col["c_v"] % cw == 0
    return pl.pallas_call(
        functools.partial(_gmlp_kernel, n_chunks=tm // MIX_CHUNK, groups=groups),
        out_shape=jax.ShapeDtypeStruct((t, cw), BF16),
        grid=(t // tm,),
        in_specs=[pl.BlockSpec((tm, cw), lambda i: (i, col["c_u"] // cw)),
                  pl.BlockSpec((tm, cw), lambda i: (i, col["c_v"] // cw)),
                  pl.BlockSpec((1, cw), lambda i: (0, 0)),
                  pl.BlockSpec((groups, MIX_CHUNK, MIX_CHUNK), lambda i: (0, 0, 0)),
                  pl.BlockSpec((groups, MIX_CHUNK, LANE), lambda i: (0, 0, 0))],
        out_specs=pl.BlockSpec((tm, cw), lambda i: (i, 0)),
        compiler_params=_params(("parallel",)),
        name="gmlp",
    )(p_all, p_all, norm_g.reshape(1, cw), ws_bf, bias)


def _route_rows(logit_rows):
    per = len(logit_rows) // N_GROUPS
    m = functools.reduce(jnp.maximum, logit_rows)
    p = [jnp.exp(r - m) for r in logit_rows]

    def top2_sum(a):
        pairs = [a[i] + a[j] for i in range(len(a)) for j in range(i + 1, len(a))]
        return functools.reduce(jnp.maximum, pairs)

    scores = [top2_sum(p[g * per:(g + 1) * per]) for g in range(N_GROUPS)]
    best, gi = scores[0], jnp.zeros(m.shape, jnp.int32)
    for g in range(1, N_GROUPS):
        better = scores[g] > best
        gi = jnp.where(better, g, gi)
        best = jnp.where(better, scores[g], best)
    a = []
    for j in range(per):
        aj = p[(N_GROUPS - 1) * per + j]
        for g in reversed(range(N_GROUPS - 1)):
            aj = jnp.where(gi == g, p[g * per + j], aj)
        a.append(aj)
    v1, i1 = a[0], jnp.zeros(m.shape, jnp.int32)
    for j in range(1, per):
        better = a[j] > v1
        i1 = jnp.where(better, j, i1)
        v1 = jnp.where(better, a[j], v1)
    v2, i2 = jnp.full(m.shape, -1.0, F32), jnp.zeros(m.shape, jnp.int32)
    for j in range(per):
        better = (i1 != j) & (a[j] > v2)
        i2 = jnp.where(better, j, i2)
        v2 = jnp.where(better, a[j], v2)
    inv = 1.0 / (v1 + v2)
    return gi * per + i1, gi * per + i2, v1 * inv, v2 * inv


def _merge_kernel(x_ref, ya_ctx_ref, ya_lat_ref, yb_ctx_ref, yb_lat_ref, yc_ref, ga0, ga1, gb0, gb1, gc0, gc1,
                  wpa_ref, wpb_ref, wpc_ref, wout_ref, g1_ref, n2_ref, sh2_ref, sc2_ref, rpair_ref, rb_ref,
                  x1_ref, h2_ref, rt_ref, cnt_ref, *, n_experts, ctx_tiles):
    def gate(r0, r1):
        return jnp.concatenate([_sigmoid(r0[...].astype(F32)), _sigmoid(r1[...].astype(F32))], axis=-1)

    is_ctx = pl.program_id(0) < ctx_tiles
    ya = jnp.where(is_ctx, ya_ctx_ref[...], ya_lat_ref[...])
    yb = jnp.where(is_ctx, yb_ctx_ref[...], yb_lat_ref[...])
    merged = gate(ga0, ga1) * jnp.dot(ya, wpa_ref[...], preferred_element_type=F32)
    merged += gate(gb0, gb1) * jnp.dot(yb, wpb_ref[...], preferred_element_type=F32)
    merged += gate(gc0, gc1) * jnp.dot(yc_ref[...], wpc_ref[...], preferred_element_type=F32)
    out = jnp.dot(merged.astype(BF16), wout_ref[...], preferred_element_type=F32)
    x1 = x_ref[...] + g1_ref[...] * out
    x1_ref[...] = x1
    h2 = _rms(x1) * n2_ref[...] * (1.0 + sc2_ref[...]) + sh2_ref[...]
    h2_ref[...] = h2.astype(BF16)
    hi, lo = _split2(h2)
    r_pair = rpair_ref[...]
    parts = (jnp.dot(hi, r_pair, preferred_element_type=F32) + jnp.dot(lo, r_pair, preferred_element_type=F32)).T
    lt = parts[0:n_experts, :] + parts[n_experts:2 * n_experts, :] + rb_ref[...]
    tm = lt.shape[1]
    e1, e2, w1, w2 = _route_rows([lt[e:e + 1, :] for e in range(n_experts)])

    @pl.when(pl.program_id(0) == 0)
    def _():
        cnt_ref[...] = jnp.zeros_like(cnt_ref)

    e_id = lax.broadcasted_iota(jnp.int32, (n_experts, tm), 0)
    hit1, hit2 = e_id == e1, e_id == e2
    onehot = (hit1 | hit2).astype(BF16)
    src = lax.broadcasted_iota(jnp.int32, (tm, tm), 0)
    dst = lax.broadcasted_iota(jnp.int32, (tm, tm), 1)
    before = jnp.dot(onehot, (src < dst).astype(BF16), preferred_element_type=F32) + cnt_ref[...]
    rank1 = jnp.sum(jnp.where(hit1, before, 0.0), axis=0, keepdims=True)
    rank2 = jnp.sum(jnp.where(hit2, before, 0.0), axis=0, keepdims=True)
    cnt_ref[...] += jnp.dot(onehot, jnp.ones((tm, tm), BF16), preferred_element_type=F32)
    zero = jnp.zeros_like(w1)
    rt_ref[...] = jnp.concatenate([e1.astype(F32), e2.astype(F32), w1, w2, rank1, rank2, zero, zero], axis=0)


ROUTE_ROWS = 8


def _merge(rows, x, y_a, y_b, y_c, p_all, col, mod_l, norm2_g, w_pa, w_pb, w_pc, w_out, layer, r_pair,
           r_bias, tm):
    t, d = x.shape
    n_experts = r_bias.shape[0]
    gw = d // 2
    assert col["gates"] % gw == 0
    g0 = col["gates"] // gw
    ctx_tiles = rows.n_ctx // tm
    row = lambda w: pl.BlockSpec((tm, w), lambda i: (i, 0))
    ctx_row = lambda w: pl.BlockSpec((tm, w), lambda i: (jnp.minimum(i, ctx_tiles - 1), 0))
    lat_row = lambda w: pl.BlockSpec((tm, w), lambda i: (jnp.maximum(i - ctx_tiles, 0), 0))
    full = lambda a: pl.BlockSpec(a.shape, lambda i: (0,) * a.ndim, pipeline_mode=pl.Buffered(1))
    of_layer = lambda a: pl.BlockSpec((None,) + a.shape[1:], lambda i: (layer,) + (0,) * (a.ndim - 1),
                                      pipeline_mode=pl.Buffered(1))
    gate_specs = [pl.BlockSpec((tm, gw), functools.partial(lambda i, k: (i, g0 + k), k=k)) for k in range(6)]
    return pl.pallas_call(
        functools.partial(_merge_kernel, n_experts=n_experts, ctx_tiles=ctx_tiles),
        out_shape=[jax.ShapeDtypeStruct((t, d), F32), jax.ShapeDtypeStruct((t, d), BF16),
                   jax.ShapeDtypeStruct((ROUTE_ROWS, t), F32), jax.ShapeDtypeStruct((n_experts, tm), F32)],
        grid=(t // tm,),
        in_specs=[row(d), ctx_row(y_a[0].shape[1]), lat_row(y_a[1].shape[1]), ctx_row(y_b[0].shape[1]),
                  lat_row(y_b[1].shape[1]), row(y_c.shape[1])] + gate_specs
                 + [of_layer(w_pa), of_layer(w_pb), of_layer(w_pc), of_layer(w_out), rows.mod_spec(2, tm, d),
                    pl.BlockSpec((1, d), lambda i: (0, 0)), rows.mod_spec(3, tm, d), rows.mod_spec(4, tm, d),
                    full(r_pair), pl.BlockSpec((n_experts, tm), lambda i: (0, 0))],
        out_specs=[row(d), row(d), pl.BlockSpec((ROUTE_ROWS, tm), lambda i: (0, i)),
                   pl.BlockSpec((n_experts, tm), lambda i: (0, 0))],
        compiler_params=_params(("arbitrary",)),
        name="merge_out_proj",
    )(x, *y_a, *y_b, y_c, *([p_all] * 6), w_pa, w_pb, w_pc, w_out, mod_l, norm2_g.reshape(1, d), mod_l, mod_l,
      r_pair, jnp.broadcast_to(r_bias.astype(F32)[:, None], (n_experts, tm)))


def _moe_kernel(be_ref, bv_ref, x_ref, wg_ref, wu_ref, wd_ref, *rest):
    o_ref = rest[-1]
    i = pl.program_id(0)

    @pl.when(bv_ref[i] > 0)
    def _():
        x = x_ref[...]
        a = jnp.dot(x, wg_ref[...], preferred_element_type=F32)
        b = jnp.dot(x, wu_ref[...], preferred_element_type=F32)
        h = (_silu(a) * b).astype(BF16)
        o_ref[...] = jnp.dot(h, wd_ref[...], preferred_element_type=F32).astype(BF16)

    @pl.when(bv_ref[i] == 0)
    def _():
        o_ref[...] = jnp.zeros_like(o_ref)


def _moe_ffn(xb, block_e, block_valid, wg, wu, wd, layer, tm, n_slots, block0=0, out_prev=None):
    n_rows, d = xb.shape
    ff = wg.shape[3]
    in_specs = [pl.BlockSpec((tm, d), lambda i, be, bv: (i, 0)),
                pl.BlockSpec((None, None, d, ff), lambda i, be, bv: (layer, be[i], 0, 0)),
                pl.BlockSpec((None, None, d, ff), lambda i, be, bv: (layer, be[i], 0, 0)),
                pl.BlockSpec((None, None, ff, d), lambda i, be, bv: (layer, be[i], 0, 0))]
    args = [block_e, block_valid, xb, wg, wu, wd]
    aliases = {}
    if out_prev is not None:
        in_specs.append(pl.BlockSpec(memory_space=pl.ANY))
        aliases = {len(args): 0}
        args.append(out_prev)
    grid_spec = pltpu.PrefetchScalarGridSpec(
        num_scalar_prefetch=2,
        grid=(n_rows // tm,),
        in_specs=in_specs,
        out_specs=pl.BlockSpec((tm, d), lambda i, be, bv: (block0 + i, 0)),
    )
    return pl.pallas_call(
        _moe_kernel,
        out_shape=jax.ShapeDtypeStruct((n_slots, d), BF16),
        grid_spec=grid_spec,
        input_output_aliases=aliases,
        compiler_params=_params(("arbitrary",)),
        name="moe_experts",
    )(*args)


def _dispatch(expert, rank, counts, tm):
    n_experts = counts.shape[0]
    n_tok = expert.shape[1]
    n_assign = TOP_K * n_tok
    padded = (counts + tm - 1) // tm * tm
    pad_end = jnp.cumsum(padded)
    pad_start = pad_end - padded
    first_slot = jnp.sum(jnp.where(expert[..., None] == jnp.arange(n_experts), pad_start, 0), axis=-1)
    dest = first_slot + rank
    n_blocks = -(-(n_assign + n_experts * (tm - 1)) // tm)
    tok = jnp.broadcast_to(jnp.arange(n_tok, dtype=jnp.int32)[None, :], dest.shape)
    slot_tok = jnp.zeros((n_blocks * tm,), jnp.int32).at[dest.reshape(-1)].set(
        tok.reshape(-1), unique_indices=True, mode="promise_in_bounds")
    block_start = jnp.arange(n_blocks, dtype=jnp.int32) * tm
    block_e = jnp.minimum(jnp.sum(block_start[:, None] >= pad_end[None, :], axis=1), n_experts - 1)
    block_valid = (block_start < pad_end[-1]).astype(jnp.int32)
    return slot_tok, dest, block_e.astype(jnp.int32), block_valid


def _final_kernel(x_ref, moe_ref, g2_ref, nf_ref, o_ref):
    x = x_ref[...] + g2_ref[...] * moe_ref[...].astype(F32)
    o_ref[...] = _rms(x) * nf_ref[...]


def _final_norm(rows, x, moe, mod_l, norm_f, row0, n_rows, tm):
    d = x.shape[1]
    t0 = row0 // tm
    row_spec = pl.BlockSpec((tm, d), lambda i: (t0 + i, 0))
    g2_spec = pl.BlockSpec((None, 1, d), lambda i: (rows.cond_row(t0 + i, tm) * N_MOD + 5, 0, 0))
    return pl.pallas_call(
        _final_kernel,
        out_shape=jax.ShapeDtypeStruct((n_rows, d), F32),
        grid=(n_rows // tm,),
        in_specs=[row_spec, row_spec, g2_spec, pl.BlockSpec((1, d), lambda i: (0, 0))],
        out_specs=pl.BlockSpec((tm, d), lambda i: (i, 0)),
        compiler_params=_params(("parallel",)),
        name="final_norm",
    )(x, moe, mod_l, norm_f.reshape(1, d))


def kernel(x_prompt, x_sample, cache_k, cache_v, state_hgrn, c, c_ctx, w_mod, b_mod, norm1, norm2, norm_f,
           w_in, hgrn_lb, hgrn_norm, q_norm, k_norm, gmlp_norm, gmlp_ws, gmlp_bs, w_pa, w_pb, w_pc, w_out,
           router_w, router_b, exp_gate, exp_up, exp_down):
    n_ctx_seq, ctx_len, d = x_prompt.shape
    n_lat_seq, lat_len, _ = x_sample.shape
    depth = w_mod.shape[0]
    past, kvh = cache_k.shape[2], cache_k.shape[3]
    a_heads = state_hgrn.shape[3]
    a_width, b_width, c_width = hgrn_lb.shape[2], w_pb.shape[1], w_pc.shape[1]
    b_heads = b_width // LANE
    n_experts = router_w.shape[1]
    assert cache_k.shape[4] == LANE and state_hgrn.shape[4:] == (LANE, LANE) and a_width == a_heads * LANE

    sizes = [("a_q", a_width), ("a_ff", a_width), ("a_fb", a_width), ("a_i", a_width), ("a_g", a_width),
             ("b_q", b_width), ("b_k", kvh * LANE), ("b_v", kvh * LANE), ("c_u", c_width), ("c_v", c_width),
             ("gates", 3 * d)]
    col, acc = {}, 0
    for name, width in sizes:
        col[name] = acc
        acc += width
    assert acc == w_in.shape[2]

    n_ctx = n_ctx_seq * ctx_len
    rows = _Rows(n_ctx, n_lat_seq, lat_len)
    tm_proj = min(1024, n_ctx, lat_len)
    tn_proj = min(1024, math.gcd(w_in.shape[2], 1024))
    tm_row = min(256, n_ctx, lat_len)
    tm_moe = 256
    tq_lat = min(256, lat_len)

    cond = jnp.concatenate([c_ctx[None, :], c, jnp.zeros((COND_ROWS - 1 - n_lat_seq, d), F32)], axis=0)
    mod = _modulation(cond, w_mod, b_mod).reshape(depth, COND_ROWS * N_MOD, 1, d)

    p_lb = jax.nn.softmax(hgrn_lb.astype(F32), axis=1)
    lbs = jnp.cumsum(p_lb, axis=1) - p_lb[:, :1]

    cos, sin, swap = _rope_tables(lat_len)
    cache_k2 = cache_k.reshape(n_lat_seq, depth, past, kvh * LANE)
    cache_v2 = cache_v.reshape(n_lat_seq, depth, past, kvh * LANE)
    s0_ctx = jnp.zeros((n_ctx_seq,) + state_hgrn.shape[2:], F32)
    assert 2 * n_experts <= LANE and n_experts % SUBLANES == 0
    r_hi, r_lo = _split2(router_w.astype(F32))
    r_pair = jnp.zeros((d, LANE), BF16).at[:, :n_experts].set(r_hi).at[:, n_experts:2 * n_experts].set(r_lo)
    gmlp_bias = jnp.broadcast_to(gmlp_bs[:, :, :, None], gmlp_bs.shape + (LANE,))

    w_in_bf, w_pa_bf, w_pb_bf, w_pc_bf, w_out_bf, gate_bf, up_bf, down_bf = (
        w.astype(BF16) for w in (w_in, w_pa, w_pb, w_pc, w_out, exp_gate, exp_up, exp_down))

    x = jnp.concatenate([x_prompt.reshape(n_ctx, d), x_sample.reshape(n_lat_seq * lat_len, d)], axis=0)
    moe = None
    new_k, new_v, new_s = [], [], []
    for l in range(depth):
        mod_l = mod[l]
        p_all, x = _in_proj(rows, x, moe, mod[l - 1] if l else None, mod_l, norm1[l], w_in_bf, l,
                            tm_proj, tn_proj)
        yb_ctx, k_l, v_l = _attention(p_all, col, 0, n_ctx_seq, ctx_len, b_heads, kvh, q_norm[l], k_norm[l],
                                      ctx_len)
        (yb_lat,) = _attention(p_all, col, n_ctx, n_lat_seq, lat_len, b_heads, kvh, q_norm[l], k_norm[l],
                               tq_lat, ctx=(cache_k2, cache_v2, l, cos, sin, swap))
        ya_ctx, s_l = _hgrn(p_all, col, 0, n_ctx_seq, ctx_len, a_heads, lbs[:, l], hgrn_norm[l], s0_ctx)
        ya_lat, _ = _hgrn(p_all, col, n_ctx, n_lat_seq, lat_len, a_heads, lbs[:, l], hgrn_norm[l],
                          state_hgrn[:, l])
        y_c = _gmlp(p_all, col, gmlp_norm[l], gmlp_ws[l].astype(BF16), gmlp_bias[l], tm_proj)
        x, h2, route, counts = _merge(rows, x, (ya_ctx, ya_lat), (yb_ctx, yb_lat), y_c, p_all, col, mod_l,
                                      norm2[l], w_pa_bf, w_pb_bf, w_pc_bf, w_out_bf, l, r_pair, router_b,
                                      tm_row)
        slot_tok, dest, block_e, block_valid = _dispatch(
            route[0:2].astype(jnp.int32), route[4:6].astype(jnp.int32), counts[:, 0].astype(jnp.int32), tm_moe)
        rows_of = lambda a, idx: a.at[idx].get(mode="promise_in_bounds")
        n_slots = slot_tok.shape[0]
        cut = n_slots // tm_moe // 2
        yb = None
        for b0, b1 in ((0, cut), (cut, n_slots // tm_moe)):
            yb = _moe_ffn(rows_of(h2, slot_tok[b0 * tm_moe:b1 * tm_moe]), block_e[b0:b1], block_valid[b0:b1],
                          gate_bf, up_bf, down_bf, l, tm_moe, n_slots, block0=b0, out_prev=yb)
        moe = (rows_of(yb, dest[0]).astype(F32) * route[2][:, None]
               + rows_of(yb, dest[1]).astype(F32) * route[3][:, None]).astype(BF16)
        new_k.append(k_l)
        new_v.append(v_l)
        new_s.append(s_l)

    y_ctx = _final_norm(rows, x, moe, mod[depth - 1], norm_f, 0, n_ctx, tm_row)
    y_lat = _final_norm(rows, x, moe, mod[depth - 1], norm_f, n_ctx, n_lat_seq * lat_len, tm_row)
    kv_shape = (n_ctx_seq, depth, ctx_len, kvh, LANE)
    return (y_ctx.reshape(n_ctx_seq, ctx_len, d), y_lat.reshape(n_lat_seq, lat_len, d),
            jnp.stack(new_k, axis=1).reshape(kv_shape), jnp.stack(new_v, axis=1).reshape(kv_shape),
            jnp.stack(new_s, axis=1))
```

```python
import functools
import math

import jax
import jax.numpy as jnp
import numpy as np
from jax import lax
from jax.experimental import pallas as pl
from jax.experimental.pallas import tpu as pltpu

F32 = jnp.float32
BF16 = jnp.bfloat16

EPS = 1e-6
N_MOD = 6
GRID_W = 64
ROPE_THETA = 10000.0
MIX_CHUNK = 128
N_GROUPS = 4
TOP_K = 2
LANE = 128
SUBLANES = 8
HGRN_BLOCK = 128
HGRN_LEVELS = 7
ATTN_KEY_CHUNK = 1024
MOE_RANGES = 4
LOG2E = 1.4426950408889634
COND_ROWS = 16
VMEM_LIMIT = 56 * 1024 * 1024

NT_DIMS = (((1,), (1,)), ((), ()))
TN_DIMS = (((0,), (0,)), ((), ()))


def _sigmoid(x):
    return 1.0 / (1.0 + jnp.exp(-x))


def _silu(x):
    return x * _sigmoid(x)


def _gelu_tanh(x):
    return 0.5 * x * (1.0 + jnp.tanh(math.sqrt(2.0 / math.pi) * (x + 0.044715 * (x * x * x))))


def _rms(x):
    return x * lax.rsqrt(jnp.mean(x * x, axis=-1, keepdims=True) + EPS)


def _split2(x):
    hi = x.astype(BF16)
    lo = (x - hi.astype(F32)).astype(BF16)
    return hi, lo


def _split3(x):
    hi = x.astype(BF16)
    r = x - hi.astype(F32)
    mid = r.astype(BF16)
    lo = (r - mid.astype(F32)).astype(BF16)
    return hi, mid, lo


def _params(sem):
    return pltpu.CompilerParams(dimension_semantics=sem, vmem_limit_bytes=VMEM_LIMIT)


def _mod_kernel(c_ref, w_ref, b_ref, o_ref):
    s = _silu(c_ref[...]).astype(BF16)
    o_ref[...] = jnp.dot(s, w_ref[...].astype(BF16), preferred_element_type=F32) + b_ref[...]


def _modulation(cond, w_mod, b_mod):
    depth, d, n = w_mod.shape
    tn = math.gcd(n, 1024)
    return pl.pallas_call(
        _mod_kernel,
        out_shape=jax.ShapeDtypeStruct((depth, COND_ROWS, n), F32),
        grid=(depth, n // tn),
        in_specs=[
            pl.BlockSpec((COND_ROWS, d), lambda l, j: (0, 0)),
            pl.BlockSpec((None, d, tn), lambda l, j: (l, 0, j)),
            pl.BlockSpec((None, 1, tn), lambda l, j: (l, 0, j)),
        ],
        out_specs=pl.BlockSpec((None, COND_ROWS, tn), lambda l, j: (l, 0, j)),
        compiler_params=_params(("parallel", "parallel")),
        name="adaln_modulation",
    )(cond, w_mod, b_mod.reshape(depth, 1, n))


class _Rows:
    def __init__(self, n_ctx, n_lat_seq, lat_len):
        self.n_ctx, self.lat_len = n_ctx, lat_len
        self.total = n_ctx + n_lat_seq * lat_len

    def cond_row(self, i, tm):
        ctx_tiles = self.n_ctx // tm
        per_seq = self.lat_len // tm
        return jnp.where(i < ctx_tiles, 0, 1 + (i - ctx_tiles) // per_seq)

    def mod_spec(self, k, tm, d):
        return pl.BlockSpec((None, 1, d), lambda i, *_: (self.cond_row(i, tm) * N_MOD + k, 0, 0))


def _in_proj_kernel(*refs, combine):
    if combine:
        x_ref, moe_ref, g2_ref, n_ref, sh_ref, sc_ref, w_ref, o_ref, x2_ref, h_scr = refs
    else:
        x_ref, n_ref, sh_ref, sc_ref, w_ref, o_ref, h_scr = refs

    @pl.when(pl.program_id(1) == 0)
    def _():
        x = x_ref[...]
        if combine:
            x = x + g2_ref[...] * moe_ref[...].astype(F32)
            x2_ref[...] = x
        h = _rms(x) * n_ref[...] * (1.0 + sc_ref[...]) + sh_ref[...]
        h_scr[...] = h.astype(BF16)

    o_ref[...] = jnp.dot(h_scr[...], w_ref[...], preferred_element_type=F32).astype(BF16)


def _in_proj(rows, x, moe, mod_prev, mod_l, norm_g, w_bf, layer, tm, tn):
    t, d = x.shape
    n = w_bf.shape[2]
    combine = moe is not None
    row_spec = pl.BlockSpec((tm, d), lambda i, j: (i, 0))
    row_once_spec = pl.BlockSpec((tm, d), lambda i, j: (i, 0), pipeline_mode=pl.Buffered(1))
    vec_spec = pl.BlockSpec((1, d), lambda i, j: (0, 0))
    in_specs = [row_once_spec]
    args = [x]
    if combine:
        in_specs += [row_once_spec, rows.mod_spec(5, tm, d)]
        args += [moe, mod_prev]
    in_specs += [vec_spec, rows.mod_spec(0, tm, d), rows.mod_spec(1, tm, d),
                 pl.BlockSpec((None, d, tn), lambda i, j: (layer, 0, j))]
    args += [norm_g.reshape(1, d), mod_l, mod_l, w_bf]
    out_shape = [jax.ShapeDtypeStruct((t, n), BF16)]
    out_specs = [pl.BlockSpec((tm, tn), lambda i, j: (i, j))]
    if combine:
        out_shape.append(jax.ShapeDtypeStruct((t, d), F32))
        out_specs.append(row_spec)
    res = pl.pallas_call(
        functools.partial(_in_proj_kernel, combine=combine),
        out_shape=out_shape,
        grid=(t // tm, n // tn),
        in_specs=in_specs,
        out_specs=out_specs,
        scratch_shapes=[pltpu.VMEM((tm, d), BF16)],
        compiler_params=_params(("parallel", "arbitrary")),
        name="in_proj",
    )(*args)
    return (res[0], res[1]) if combine else (res[0], x)


def _rope(x, cos, sin_signed, swap):
    swapped = jnp.dot(x.astype(BF16), swap, preferred_element_type=F32)
    return x * cos + swapped * sin_signed


def _attn_kernel(*refs, latent, past, tq, grp, scale):
    if latent:
        (q_ref, kn_ref, vn_ref, ck_ref, cv_ref, cos_ref, sin_ref, swap_ref, qg_ref, kg_ref,
         o_ref, k_scr, v_scr) = refs
    else:
        q_ref, kn_ref, vn_ref, qg_ref, kg_ref, o_ref, ko_ref, vo_ref, k_scr, v_scr = refs
    qi = pl.program_id(2)

    @pl.when(qi == 0)
    def _():
        k = _rms(kn_ref[...].astype(F32)) * kg_ref[...]
        if latent:
            k = _rope(k, cos_ref[...], sin_ref[...], swap_ref[...])
            k_scr[0:past, :] = ck_ref[...].astype(BF16)
            v_scr[0:past, 0:LANE] = cv_ref[...].astype(BF16)
        else:
            ko_ref[...] = k
            vo_ref[...] = vn_ref[...].astype(F32)
        k_scr[past:, :] = k.astype(BF16)
        v_scr[past:, 0:LANE] = vn_ref[...]
        v_scr[:, LANE:] = jnp.ones((v_scr.shape[0], LANE), BF16)

    q = q_ref[...].astype(F32)
    heads = []
    for h in range(grp):
        qh = _rms(q[:, h * LANE:(h + 1) * LANE]) * qg_ref[...]
        if latent:
            r0 = pl.multiple_of(qi * tq, tq)
            qh = _rope(qh, cos_ref[pl.ds(r0, tq), :], sin_ref[pl.ds(r0, tq), :], swap_ref[...])
        heads.append((qh * scale).astype(BF16))
    qs = jnp.concatenate(heads, axis=0)
    n_keys = k_scr.shape[0]
    chunks = [(c0, min(c0 + ATTN_KEY_CHUNK, n_keys)) for c0 in range(0, n_keys, ATTN_KEY_CHUNK)]

    def scores(c):
        return lax.dot_general(qs, k_scr[c[0]:c[1], :], NT_DIMS, preferred_element_type=F32)

    m = acc = None
    s_next = scores(chunks[0])
    for n, (c0, c1) in enumerate(chunks):
        s = s_next
        if n + 1 < len(chunks):
            s_next = scores(chunks[n + 1])
        m_c = jnp.max(functools.reduce(jnp.maximum, [s[:, t:t + LANE] for t in range(0, c1 - c0, LANE)]),
                      axis=-1, keepdims=True)
        m_new = m_c if m is None else jnp.maximum(m, m_c)
        pv = jnp.dot(jnp.exp2(s - m_new).astype(BF16), v_scr[c0:c1, :], preferred_element_type=F32)
        acc = pv if m is None else acc * jnp.exp2(m - m_new) + pv
        m = m_new
    o = acc[:, 0:LANE] / acc[:, LANE:]
    for h in range(grp):
        o_ref[:, h * LANE:(h + 1) * LANE] = o[h * tq:(h + 1) * tq, :].astype(BF16)


def _attention(p_all, col, row0, n_seq, seq_len, n_heads, kvh, q_gain, k_gain, tq, ctx=None):
    grp = n_heads // kvh
    latent = ctx is not None
    qw = grp * LANE
    assert col["b_q"] % qw == 0 and row0 % seq_len == 0 and seq_len % tq == 0
    seq0, nq = row0 // seq_len, seq_len // tq
    q_spec = pl.BlockSpec((tq, qw), lambda b, g, i: (seq0 * nq + b * nq + i, col["b_q"] // qw + g))
    k_spec = pl.BlockSpec((seq_len, LANE), lambda b, g, i: (seq0 + b, col["b_k"] // LANE + g))
    v_spec = pl.BlockSpec((seq_len, LANE), lambda b, g, i: (seq0 + b, col["b_v"] // LANE + g))
    gain_spec = pl.BlockSpec((1, LANE), lambda b, g, i: (0, 0))
    in_specs = [q_spec, k_spec, v_spec]
    args = [p_all, p_all, p_all]
    past = 0
    if latent:
        cache_k, cache_v, layer, cos, sin, swap = ctx
        past = cache_k.shape[2]
        c_spec = pl.BlockSpec((None, None, past, LANE), lambda b, g, i: (b, layer, 0, g))
        t_spec = pl.BlockSpec((seq_len, LANE), lambda b, g, i: (0, 0))
        in_specs += [c_spec, c_spec, t_spec, t_spec, pl.BlockSpec((LANE, LANE), lambda b, g, i: (0, 0))]
        args += [cache_k, cache_v, cos, sin, swap]
    in_specs += [gain_spec, gain_spec]
    args += [q_gain.reshape(1, LANE), k_gain.reshape(1, LANE)]
    out_shape = [jax.ShapeDtypeStruct((n_seq * seq_len, n_heads * LANE), BF16)]
    out_specs = [pl.BlockSpec((tq, qw), lambda b, g, i: (b * nq + i, g))]
    if not latent:
        kv_shape = jax.ShapeDtypeStruct((n_seq, seq_len, kvh * LANE), F32)
        kv_spec = pl.BlockSpec((None, seq_len, LANE), lambda b, g, i: (b, 0, g))
        out_shape += [kv_shape, kv_shape]
        out_specs += [kv_spec, kv_spec]
    return pl.pallas_call(
        functools.partial(_attn_kernel, latent=latent, past=past, tq=tq, grp=grp, scale=LANE ** -0.5 * LOG2E),
        out_shape=out_shape,
        grid=(n_seq, kvh, nq),
        in_specs=in_specs,
        out_specs=out_specs,
        scratch_shapes=[pltpu.VMEM((past + seq_len, LANE), BF16), pltpu.VMEM((past + seq_len, 2 * LANE), BF16)],
        compiler_params=_params(("parallel", "parallel", "arbitrary")),
        name="attention_latent" if latent else "attention_context",
    )(*args)


def _rope_tables(seq_len):
    quarter = LANE // 4
    inv = ROPE_THETA ** (-jnp.arange(quarter, dtype=F32) / quarter)
    pos = jnp.arange(seq_len)
    row = (pos // GRID_W).astype(F32)[:, None] * inv[None, :]
    colp = (pos % GRID_W).astype(F32)[:, None] * inv[None, :]
    cos = jnp.concatenate([jnp.cos(row), jnp.cos(row), jnp.cos(colp), jnp.cos(colp)], axis=-1)
    sin = jnp.concatenate([-jnp.sin(row), jnp.sin(row), -jnp.sin(colp), jnp.sin(colp)], axis=-1)
    lane = np.arange(LANE)
    partner = np.where((lane // quarter) % 2 == 0, lane + quarter, lane - quarter)
    swap = jnp.asarray(lane[:, None] == partner[None, :], BF16)
    return cos, sin, swap


def _hgrn_level_codes():
    r = np.arange(HGRN_BLOCK)[:, None]
    u = np.arange(HGRN_BLOCK)[None, :]
    diff = r ^ u
    code = np.where(diff == 0, 0, 1 + np.floor(np.log2(np.maximum(diff, 1))).astype(np.int32))
    return [jnp.asarray(np.where(keep, code, -1), jnp.int32) for keep in (r >= u, r <= u)]


def _segment_runs(x):
    n_tiles = x.shape[0] // SUBLANES
    sub = lax.broadcasted_iota(jnp.int32, (SUBLANES, x.shape[1]), 0)
    pre_t, suf_t = [], []
    for i in range(n_tiles):
        p = s = x[i * SUBLANES:(i + 1) * SUBLANES]
        pre_i, suf_i = [], []
        n = 1
        while n < SUBLANES:
            pos = sub % (2 * n)
            add_p, add_s = jnp.zeros_like(p), jnp.zeros_like(s)
            for j in range(n):
                add_p = jnp.where(pos == n + j, pltpu.roll(p, j + 1, 0), add_p)
                add_s = jnp.where(pos == n - 1 - j, pltpu.roll(s, SUBLANES - (j + 1), 0), add_s)
            p, s = p + add_p, s + add_s
            pre_i.append(p)
            suf_i.append(s)
            n *= 2
        pre_t.append(pre_i)
        suf_t.append(suf_i)
    levels = len(pre_t[0])
    pre = [[t[m] for t in pre_t] for m in range(levels)]
    suf = [[t[m] for t in suf_t] for m in range(levels)]
    cur_p, cur_s = pre[-1], suf[-1]
    tot = [jnp.broadcast_to(t[SUBLANES - 1:SUBLANES, :], t.shape) for t in cur_p]
    seg = 1
    while seg < n_tiles:
        new_p, new_s, new_tot = list(cur_p), list(cur_s), list(tot)
        for base in range(0, n_tiles, 2 * seg):
            t_first, t_second = tot[base], tot[base + seg]
            both = t_first + t_second
            for i in range(base, base + seg):
                new_s[i] = cur_s[i] + t_second
                new_p[i + seg] = cur_p[i + seg] + t_first
                new_tot[i] = new_tot[i + seg] = both
        cur_p, cur_s, tot = new_p, new_s, new_tot
        pre.append(cur_p)
        suf.append(cur_s)
        seg *= 2
    join = lambda tiles: jnp.concatenate(tiles, axis=0)
    return [join(t) for t in pre], [join(t) for t in suf]


def _hgrn_kernel(q_ref, ff_ref, fb_ref, i_ref, g_ref, lb_ref, ng_ref, s0_ref, lvlf_ref, lvlb_ref, y_ref,
                 sfin_ref, of_scr, ob_scr, *, seq_len):
    blk, n_lvl = HGRN_BLOCK, HGRN_LEVELS
    n_blk = seq_len // blk

    def scan_blocks(chains):
        q, k, v, q_run, k_run = [], [], [], [], []
        for r0, f_ref, lb, _, forward, _, _ in chains:
            q.append(_silu(q_ref[pl.ds(r0, blk), :].astype(F32)))
            g = lb + (1.0 - lb) * _sigmoid(f_ref[pl.ds(r0, blk), :].astype(F32))
            k.append(1.0 - g)
            v.append(i_ref[pl.ds(r0, blk), :])
            log_g = jnp.log(g) * LOG2E
            pre, suf = _segment_runs(log_g)
            near, far = (pre, suf) if forward else (suf, pre)
            q_run.append([log_g] + near)
            k_run.append([None] + [x - log_g for x in far])
        kb = [x.astype(BF16) for x in k]
        level = [c[5][...] for c in chains]
        score = [jnp.where(lv == 0, lax.dot_general(qc.astype(BF16), kc, NT_DIMS, preferred_element_type=F32), 0.0)
                 for lv, qc, kc in zip(level, q, kb)]
        for m in range(n_lvl):
            for i in range(len(chains)):
                qs = (q[i] * jnp.exp2(q_run[i][m])).astype(BF16)
                ks = kb[i] if m == 0 else (k[i] * jnp.exp2(k_run[i][m])).astype(BF16)
                score[i] = jnp.where(level[i] == m + 1,
                                     lax.dot_general(qs, ks, NT_DIMS, preferred_element_type=F32), score[i])
        out = []
        for i, (_, _, _, st, _, _, edge) in enumerate(chains):
            if isinstance(st, int):
                st = out[st][1]
            q_all = jnp.exp2(q_run[i][n_lvl])
            o = jnp.dot(score[i].astype(BF16), v[i], preferred_element_type=F32)
            o += lax.dot_general((q[i] * q_all).astype(BF16), st.astype(BF16), NT_DIMS,
                                 preferred_element_type=F32)
            kd = (k[i] * jnp.exp2(k_run[i][n_lvl])).astype(BF16)
            st = st * q_all[edge:edge + 1, :] + lax.dot_general(v[i], kd, TN_DIMS, preferred_element_type=F32)
            out.append((o, st))
        return out

    lb_f, lb_b = lb_ref[0:1, :], lb_ref[1:2, :]

    def body(j, carry):
        st_f, st_b = carry
        r_f = pl.multiple_of(2 * j * blk, 2 * blk)
        r_b = pl.multiple_of((n_blk - 2 - 2 * j) * blk, 2 * blk)
        res = scan_blocks([(r_f, ff_ref, lb_f, st_f, True, lvlf_ref, blk - 1),
                           (r_b + blk, fb_ref, lb_b, st_b, False, lvlb_ref, 0),
                           (r_f + blk, ff_ref, lb_f, 0, True, lvlf_ref, blk - 1),
                           (r_b, fb_ref, lb_b, 1, False, lvlb_ref, 0)])
        of_scr[pl.ds(r_f, blk), :] = res[0][0]
        of_scr[pl.ds(r_f + blk, blk), :] = res[2][0]
        ob_scr[pl.ds(r_b + blk, blk), :] = res[1][0]
        ob_scr[pl.ds(r_b, blk), :] = res[3][0]
        return res[2][1], res[3][1]

    st_f, st_b = lax.fori_loop(0, n_blk // 2, body, (s0_ref[0].T, s0_ref[1].T))
    sfin_ref[0] = st_f.T
    sfin_ref[1] = st_b.T
    o = _rms(of_scr[...] + ob_scr[...]) * ng_ref[...]
    y_ref[...] = (o * _silu(g_ref[...].astype(F32))).astype(BF16)


def _hgrn(p_all, col, row0, n_seq, seq_len, n_heads, lb_l, norm_g, s0):
    assert row0 % seq_len == 0 and seq_len % (2 * HGRN_BLOCK) == 0
    seq0 = row0 // seq_len

    def cspec(name):
        return pl.BlockSpec((seq_len, LANE), lambda b, h: (seq0 + b, col[name] // LANE + h))

    st_spec = pl.BlockSpec((None, 2, None, LANE, LANE), lambda b, h: (b, 0, h, 0, 0))
    tables = _hgrn_level_codes()
    return pl.pallas_call(
        functools.partial(_hgrn_kernel, seq_len=seq_len),
        out_shape=[jax.ShapeDtypeStruct((n_seq * seq_len, n_heads * LANE), BF16),
                   jax.ShapeDtypeStruct(s0.shape, F32)],
        grid=(n_seq, n_heads),
        in_specs=[cspec("a_q"), cspec("a_ff"), cspec("a_fb"), cspec("a_i"), cspec("a_g"),
                  pl.BlockSpec((2, LANE), lambda b, h: (0, h)),
                  pl.BlockSpec((1, LANE), lambda b, h: (0, 0)),
                  st_spec] + [pl.BlockSpec(t.shape, lambda b, h: (0, 0)) for t in tables],
        out_specs=[pl.BlockSpec((seq_len, LANE), lambda b, h: (b, h)), st_spec],
        scratch_shapes=[pltpu.VMEM((seq_len, LANE), F32), pltpu.VMEM((seq_len, LANE), F32)],
        compiler_params=_params(("parallel", "parallel")),
        name="hgrn_%d" % seq_len,
    )(p_all, p_all, p_all, p_all, p_all, lb_l, norm_g.reshape(1, LANE), s0, *tables)


def _gmlp_kernel(u_ref, v_ref, gn_ref, ws_ref, bs_ref, o_ref, *, n_chunks, groups):
    u = _gelu_tanh(u_ref[...].astype(F32))
    v = (_rms(_gelu_tanh(v_ref[...].astype(F32))) * gn_ref[...]).astype(BF16)
    for n in range(n_chunks):
        rs = slice(n * MIX_CHUNK, (n + 1) * MIX_CHUNK)
        for g in range(groups):
            cs = slice(g * LANE, (g + 1) * LANE)
            mixed = jnp.dot(ws_ref[g], v[rs, cs], preferred_element_type=F32) + bs_ref[g]
            o_ref[rs, cs] = (u[rs, cs] * mixed).astype(BF16)


def _gmlp(p_all, col, norm_g, ws_bf, bias, tm):
    t = p_all.shape[0]
    groups = ws_bf.shape[0]
    cw = groups * LANE
    assert col["c_u"] % cw == 0 and col["c_v"] % cw == 0
    return pl.pallas_call(
        functools.partial(_gmlp_kernel, n_chunks=tm // MIX_CHUNK, groups=groups),
        out_shape=jax.ShapeDtypeStruct((t, cw), BF16),
        grid=(t // tm,),
        in_specs=[pl.BlockSpec((tm, cw), lambda i: (i, col["c_u"] // cw)),
                  pl.BlockSpec((tm, cw), lambda i: (i, col["c_v"] // cw)),
                  pl.BlockSpec((1, cw), lambda i: (0, 0)),
                  pl.BlockSpec((groups, MIX_CHUNK, MIX_CHUNK), lambda i: (0, 0, 0)),
                  pl.BlockSpec((groups, MIX_CHUNK, LANE), lambda i: (0, 0, 0))],
        out_specs=pl.BlockSpec((tm, cw), lambda i: (i, 0)),
        compiler_params=_params(("parallel",)),
        name="gmlp",
    )(p_all, p_all, norm_g.reshape(1, cw), ws_bf, bias)


def _route_rows(logit_rows):
    per = len(logit_rows) // N_GROUPS
    m = functools.reduce(jnp.maximum, logit_rows)
    p = [jnp.exp(r - m) for r in logit_rows]

    def top2_sum(a):
        pairs = [a[i] + a[j] for i in range(len(a)) for j in range(i + 1, len(a))]
        return functools.reduce(jnp.maximum, pairs)

    scores = [top2_sum(p[g * per:(g + 1) * per]) for g in range(N_GROUPS)]
    best, gi = scores[0], jnp.zeros(m.shape, jnp.int32)
    for g in range(1, N_GROUPS):
        better = scores[g] > best
        gi = jnp.where(better, g, gi)
        best = jnp.where(better, scores[g], best)
    a = []
    for j in range(per):
        aj = p[(N_GROUPS - 1) * per + j]
        for g in reversed(range(N_GROUPS - 1)):
            aj = jnp.where(gi == g, p[g * per + j], aj)
        a.append(aj)
    v1, i1 = a[0], jnp.zeros(m.shape, jnp.int32)
    for j in range(1, per):
        better = a[j] > v1
        i1 = jnp.where(better, j, i1)
        v1 = jnp.where(better, a[j], v1)
    v2, i2 = jnp.full(m.shape, -1.0, F32), jnp.zeros(m.shape, jnp.int32)
    for j in range(per):
        better = (i1 != j) & (a[j] > v2)
        i2 = jnp.where(better, j, i2)
        v2 = jnp.where(better, a[j], v2)
    inv = 1.0 / (v1 + v2)
    return gi * per + i1, gi * per + i2, v1 * inv, v2 * inv


def _merge_kernel(x_ref, ya_ctx_ref, ya_lat_ref, yb_ctx_ref, yb_lat_ref, yc_ref, ga0, ga1, gb0, gb1, gc0, gc1,
                  wpa_ref, wpb_ref, wpc_ref, wout_ref, g1_ref, n2_ref, sh2_ref, sc2_ref, rpair_ref, rb_ref,
                  x1_ref, h2_ref, rt_ref, cnt_ref, *, n_experts, ctx_tiles):
    def gate(r0, r1):
        return jnp.concatenate([_sigmoid(r0[...].astype(F32)), _sigmoid(r1[...].astype(F32))], axis=-1)

    is_ctx = pl.program_id(0) < ctx_tiles
    ya = jnp.where(is_ctx, ya_ctx_ref[...], ya_lat_ref[...])
    yb = jnp.where(is_ctx, yb_ctx_ref[...], yb_lat_ref[...])
    merged = gate(ga0, ga1) * jnp.dot(ya, wpa_ref[...], preferred_element_type=F32)
    merged += gate(gb0, gb1) * jnp.dot(yb, wpb_ref[...], preferred_element_type=F32)
    merged += gate(gc0, gc1) * jnp.dot(yc_ref[...], wpc_ref[...], preferred_element_type=F32)
    out = jnp.dot(merged.astype(BF16), wout_ref[...], preferred_element_type=F32)
    x1 = x_ref[...] + g1_ref[...] * out
    x1_ref[...] = x1
    h2 = _rms(x1) * n2_ref[...] * (1.0 + sc2_ref[...]) + sh2_ref[...]
    h2_ref[...] = h2.astype(BF16)
    hi, lo = _split2(h2)
    r_pair = rpair_ref[...]
    parts = (jnp.dot(hi, r_pair, preferred_element_type=F32) + jnp.dot(lo, r_pair, preferred_element_type=F32)).T
    lt = parts[0:n_experts, :] + parts[n_experts:2 * n_experts, :] + rb_ref[...]
    tm = lt.shape[1]
    e1, e2, w1, w2 = _route_rows([lt[e:e + 1, :] for e in range(n_experts)])

    @pl.when(pl.program_id(0) == 0)
    def _():
        cnt_ref[...] = jnp.zeros_like(cnt_ref)

    e_id = lax.broadcasted_iota(jnp.int32, (n_experts, tm), 0)
    hit1, hit2 = e_id == e1, e_id == e2
    onehot = (hit1 | hit2).astype(BF16)
    src = lax.broadcasted_iota(jnp.int32, (tm, tm), 0)
    dst = lax.broadcasted_iota(jnp.int32, (tm, tm), 1)
    before = jnp.dot(onehot, (src < dst).astype(BF16), preferred_element_type=F32) + cnt_ref[...]
    rank1 = jnp.sum(jnp.where(hit1, before, 0.0), axis=0, keepdims=True)
    rank2 = jnp.sum(jnp.where(hit2, before, 0.0), axis=0, keepdims=True)
    cnt_ref[...] += jnp.dot(onehot, jnp.ones((tm, tm), BF16), preferred_element_type=F32)
    zero = jnp.zeros_like(w1)
    rt_ref[...] = jnp.concatenate([e1.astype(F32), e2.astype(F32), w1, w2, rank1, rank2, zero, zero], axis=0)


ROUTE_ROWS = 8


def _merge(rows, x, y_a, y_b, y_c, p_all, col, mod_l, norm2_g, w_pa, w_pb, w_pc, w_out, layer, r_pair,
           r_bias, tm):
    t, d = x.shape
    n_experts = r_bias.shape[0]
    gw = d // 2
    assert col["gates"] % gw == 0
    g0 = col["gates"] // gw
    ctx_tiles = rows.n_ctx // tm
    row = lambda w: pl.BlockSpec((tm, w), lambda i: (i, 0))
    ctx_row = lambda w: pl.BlockSpec((tm, w), lambda i: (jnp.minimum(i, ctx_tiles - 1), 0))
    lat_row = lambda w: pl.BlockSpec((tm, w), lambda i: (jnp.maximum(i - ctx_tiles, 0), 0))
    full = lambda a: pl.BlockSpec(a.shape, lambda i: (0,) * a.ndim, pipeline_mode=pl.Buffered(1))
    of_layer = lambda a: pl.BlockSpec((None,) + a.shape[1:], lambda i: (layer,) + (0,) * (a.ndim - 1),
                                      pipeline_mode=pl.Buffered(1))
    gate_specs = [pl.BlockSpec((tm, gw), functools.partial(lambda i, k: (i, g0 + k), k=k)) for k in range(6)]
    return pl.pallas_call(
        functools.partial(_merge_kernel, n_experts=n_experts, ctx_tiles=ctx_tiles),
        out_shape=[jax.ShapeDtypeStruct((t, d), F32), jax.ShapeDtypeStruct((t, d), BF16),
                   jax.ShapeDtypeStruct((ROUTE_ROWS, t), F32), jax.ShapeDtypeStruct((n_experts, tm), F32)],
        grid=(t // tm,),
        in_specs=[row(d), ctx_row(y_a[0].shape[1]), lat_row(y_a[1].shape[1]), ctx_row(y_b[0].shape[1]),
                  lat_row(y_b[1].shape[1]), row(y_c.shape[1])] + gate_specs
                 + [of_layer(w_pa), of_layer(w_pb), of_layer(w_pc), of_layer(w_out), rows.mod_spec(2, tm, d),
                    pl.BlockSpec((1, d), lambda i: (0, 0)), rows.mod_spec(3, tm, d), rows.mod_spec(4, tm, d),
                    full(r_pair), pl.BlockSpec((n_experts, tm), lambda i: (0, 0))],
        out_specs=[row(d), row(d), pl.BlockSpec((ROUTE_ROWS, tm), lambda i: (0, i)),
                   pl.BlockSpec((n_experts, tm), lambda i: (0, 0))],
        compiler_params=_params(("arbitrary",)),
        name="merge_out_proj",
    )(x, *y_a, *y_b, y_c, *([p_all] * 6), w_pa, w_pb, w_pc, w_out, mod_l, norm2_g.reshape(1, d), mod_l, mod_l,
      r_pair, jnp.broadcast_to(r_bias.astype(F32)[:, None], (n_experts, tm)))


def _moe_kernel(be_ref, bv_ref, x_ref, wg_ref, wu_ref, wd_ref, *rest):
    o_ref = rest[-1]
    i = pl.program_id(0)

    @pl.when(bv_ref[i] > 0)
    def _():
        x = x_ref[...]
        a = jnp.dot(x, wg_ref[...], preferred_element_type=F32)
        b = jnp.dot(x, wu_ref[...], preferred_element_type=F32)
        h = (_silu(a) * b).astype(BF16)
        o_ref[...] = jnp.dot(h, wd_ref[...], preferred_element_type=F32).astype(BF16)

    @pl.when(bv_ref[i] == 0)
    def _():
        o_ref[...] = jnp.zeros_like(o_ref)


def _moe_ffn(xb, block_e, block_valid, wg, wu, wd, layer, tm, n_slots, block0=0, out_prev=None):
    n_rows, d = xb.shape
    ff = wg.shape[3]
    in_specs = [pl.BlockSpec((tm, d), lambda i, be, bv: (i, 0)),
                pl.BlockSpec((None, None, d, ff), lambda i, be, bv: (layer, be[i], 0, 0)),
                pl.BlockSpec((None, None, d, ff), lambda i, be, bv: (layer, be[i], 0, 0)),
                pl.BlockSpec((None, None, ff, d), lambda i, be, bv: (layer, be[i], 0, 0))]
    args = [block_e, block_valid, xb, wg, wu, wd]
    aliases = {}
    if out_prev is not None:
        in_specs.append(pl.BlockSpec(memory_space=pl.ANY))
        aliases = {len(args): 0}
        args.append(out_prev)
    grid_spec = pltpu.PrefetchScalarGridSpec(
        num_scalar_prefetch=2,
        grid=(n_rows // tm,),
        in_specs=in_specs,
        out_specs=pl.BlockSpec((tm, d), lambda i, be, bv: (block0 + i, 0)),
    )
    return pl.pallas_call(
        _moe_kernel,
        out_shape=jax.ShapeDtypeStruct((n_slots, d), BF16),
        grid_spec=grid_spec,
        input_output_aliases=aliases,
        compiler_params=_params(("arbitrary",)),
        name="moe_experts",
    )(*args)


def _dispatch(expert, rank, counts, tm):
    n_experts = counts.shape[0]
    n_tok = expert.shape[1]
    n_assign = TOP_K * n_tok
    padded = (counts + tm - 1) // tm * tm
    pad_end = jnp.cumsum(padded)
    pad_start = pad_end - padded
    first_slot = jnp.sum(jnp.where(expert[..., None] == jnp.arange(n_experts), pad_start, 0), axis=-1)
    dest = first_slot + rank
    n_blocks = -(-(n_assign + n_experts * (tm - 1)) // tm)
    tok = jnp.broadcast_to(jnp.arange(n_tok, dtype=jnp.int32)[None, :], dest.shape)
    slot_tok = jnp.zeros((n_blocks * tm,), jnp.int32).at[dest.reshape(-1)].set(
        tok.reshape(-1), unique_indices=True, mode="promise_in_bounds")
    block_start = jnp.arange(n_blocks, dtype=jnp.int32) * tm
    block_e = jnp.minimum(jnp.sum(block_start[:, None] >= pad_end[None, :], axis=1), n_experts - 1)
    block_valid = (block_start < pad_end[-1]).astype(jnp.int32)
    return slot_tok, dest, block_e.astype(jnp.int32), block_valid


def _final_kernel(x_ref, moe_ref, g2_ref, nf_ref, o_ref):
    x = x_ref[...] + g2_ref[...] * moe_ref[...].astype(F32)
    o_ref[...] = _rms(x) * nf_ref[...]


def _final_norm(rows, x, moe, mod_l, norm_f, row0, n_rows, tm):
    d = x.shape[1]
    t0 = row0 // tm
    row_spec = pl.BlockSpec((tm, d), lambda i: (t0 + i, 0))
    g2_spec = pl.BlockSpec((None, 1, d), lambda i: (rows.cond_row(t0 + i, tm) * N_MOD + 5, 0, 0))
    return pl.pallas_call(
        _final_kernel,
        out_shape=jax.ShapeDtypeStruct((n_rows, d), F32),
        grid=(n_rows // tm,),
        in_specs=[row_spec, row_spec, g2_spec, pl.BlockSpec((1, d), lambda i: (0, 0))],
        out_specs=pl.BlockSpec((tm, d), lambda i: (i, 0)),
        compiler_params=_params(("parallel",)),
        name="final_norm",
    )(x, moe, mod_l, norm_f.reshape(1, d))


def kernel(x_prompt, x_sample, cache_k, cache_v, state_hgrn, c, c_ctx, w_mod, b_mod, norm1, norm2, norm_f,
           w_in, hgrn_lb, hgrn_norm, q_norm, k_norm, gmlp_norm, gmlp_ws, gmlp_bs, w_pa, w_pb, w_pc, w_out,
           router_w, router_b, exp_gate, exp_up, exp_down):
    n_ctx_seq, ctx_len, d = x_prompt.shape
    n_lat_seq, lat_len, _ = x_sample.shape
    depth = w_mod.shape[0]
    past, kvh = cache_k.shape[2], cache_k.shape[3]
    a_heads = state_hgrn.shape[3]
    a_width, b_width, c_width = hgrn_lb.shape[2], w_pb.shape[1], w_pc.shape[1]
    b_heads = b_width // LANE
    n_experts = router_w.shape[1]
    assert cache_k.shape[4] == LANE and state_hgrn.shape[4:] == (LANE, LANE) and a_width == a_heads * LANE

    sizes = [("a_q", a_width), ("a_ff", a_width), ("a_fb", a_width), ("a_i", a_width), ("a_g", a_width),
             ("b_q", b_width), ("b_k", kvh * LANE), ("b_v", kvh * LANE), ("c_u", c_width), ("c_v", c_width),
             ("gates", 3 * d)]
    col, acc = {}, 0
    for name, width in sizes:
        col[name] = acc
        acc += width
    assert acc == w_in.shape[2]

    n_ctx = n_ctx_seq * ctx_len
    rows = _Rows(n_ctx, n_lat_seq, lat_len)
    tm_proj = min(1024, n_ctx, lat_len)
    tn_proj = min(1024, math.gcd(w_in.shape[2], 1024))
    tm_row = min(256, n_ctx, lat_len)
    tm_moe = 256
    tq_lat = min(256, lat_len)

    cond = jnp.concatenate([c_ctx[None, :], c, jnp.zeros((COND_ROWS - 1 - n_lat_seq, d), F32)], axis=0)
    mod = _modulation(cond, w_mod, b_mod).reshape(depth, COND_ROWS * N_MOD, 1, d)

    p_lb = jax.nn.softmax(hgrn_lb.astype(F32), axis=1)
    lbs = jnp.cumsum(p_lb, axis=1) - p_lb[:, :1]

    cos, sin, swap = _rope_tables(lat_len)
    cache_k2 = cache_k.reshape(n_lat_seq, depth, past, kvh * LANE)
    cache_v2 = cache_v.reshape(n_lat_seq, depth, past, kvh * LANE)
    s0_ctx = jnp.zeros((n_ctx_seq,) + state_hgrn.shape[2:], F32)
    assert 2 * n_experts <= LANE and n_experts % SUBLANES == 0
    r_hi, r_lo = _split2(router_w.astype(F32))
    r_pair = jnp.zeros((d, LANE), BF16).at[:, :n_experts].set(r_hi).at[:, n_experts:2 * n_experts].set(r_lo)
    gmlp_bias = jnp.broadcast_to(gmlp_bs[:, :, :, None], gmlp_bs.shape + (LANE,))

    w_in_bf, w_pa_bf, w_pb_bf, w_pc_bf, w_out_bf, gate_bf, up_bf, down_bf = (
        w.astype(BF16) for w in (w_in, w_pa, w_pb, w_pc, w_out, exp_gate, exp_up, exp_down))

    x = jnp.concatenate([x_prompt.reshape(n_ctx, d), x_sample.reshape(n_lat_seq * lat_len, d)], axis=0)
    moe = None
    new_k, new_v, new_s = [], [], []
    for l in range(depth):
        mod_l = mod[l]
        p_all, x = _in_proj(rows, x, moe, mod[l - 1] if l else None, mod_l, norm1[l], w_in_bf, l,
                            tm_proj, tn_proj)
        yb_ctx, k_l, v_l = _attention(p_all, col, 0, n_ctx_seq, ctx_len, b_heads, kvh, q_norm[l], k_norm[l],
                                      ctx_len)
        (yb_lat,) = _attention(p_all, col, n_ctx, n_lat_seq, lat_len, b_heads, kvh, q_norm[l], k_norm[l],
                               tq_lat, ctx=(cache_k2, cache_v2, l, cos, sin, swap))
        ya_ctx, s_l = _hgrn(p_all, col, 0, n_ctx_seq, ctx_len, a_heads, lbs[:, l], hgrn_norm[l], s0_ctx)
        ya_lat, _ = _hgrn(p_all, col, n_ctx, n_lat_seq, lat_len, a_heads, lbs[:, l], hgrn_norm[l],
                          state_hgrn[:, l])
        y_c = _gmlp(p_all, col, gmlp_norm[l], gmlp_ws[l].astype(BF16), gmlp_bias[l], tm_proj)
        x, h2, route, counts = _merge(rows, x, (ya_ctx, ya_lat), (yb_ctx, yb_lat), y_c, p_all, col, mod_l,
                                      norm2[l], w_pa_bf, w_pb_bf, w_pc_bf, w_out_bf, l, r_pair, router_b,
                                      tm_row)
        slot_tok, dest, block_e, block_valid = _dispatch(
            route[0:2].astype(jnp.int32), route[4:6].astype(jnp.int32), counts[:, 0].astype(jnp.int32), tm_moe)
        rows_of = lambda a, idx: a.at[idx].get(mode="promise_in_bounds")
        n_slots = slot_tok.shape[0]
        cuts = [n_slots // tm_moe * r // MOE_RANGES for r in range(MOE_RANGES + 1)]
        yb = None
        for b0, b1 in zip(cuts[:-1], cuts[1:]):
            yb = _moe_ffn(rows_of(h2, slot_tok[b0 * tm_moe:b1 * tm_moe]), block_e[b0:b1], block_valid[b0:b1],
                          gate_bf, up_bf, down_bf, l, tm_moe, n_slots, block0=b0, out_prev=yb)
        moe = (rows_of(yb, dest[0]).astype(F32) * route[2][:, None]
               + rows_of(yb, dest[1]).astype(F32) * route[3][:, None]).astype(BF16)
        new_k.append(k_l)
        new_v.append(v_l)
        new_s.append(s_l)

    y_ctx = _final_norm(rows, x, moe, mod[depth - 1], norm_f, 0, n_ctx, tm_row)
    y_lat = _final_norm(rows, x, moe, mod[depth - 1], norm_f, n_ctx, n_lat_seq * lat_len, tm_row)
    kv_shape = (n_ctx_seq, depth, ctx_len, kvh, LANE)
    return (y_ctx.reshape(n_ctx_seq, ctx_len, d), y_lat.reshape(n_lat_seq, lat_len, d),
            jnp.stack(new_k, axis=1).reshape(kv_shape), jnp.stack(new_v, axis=1).reshape(kv_shape),
            jnp.stack(new_s, axis=1))
```

```python
import functools
import math

import jax
import jax.numpy as jnp
import numpy as np
from jax import lax
from jax.experimental import pallas as pl
from jax.experimental.pallas import tpu as pltpu

F32 = jnp.float32
BF16 = jnp.bfloat16

EPS = 1e-6
N_MOD = 6
GRID_W = 64
ROPE_THETA = 10000.0
MIX_CHUNK = 128
N_GROUPS = 4
TOP_K = 2
LANE = 128
SUBLANES = 8
HGRN_BLOCK = 128
HGRN_LEVELS = 7
ATTN_KEY_CHUNK = 2048
MOE_RANGES = 4
LOG2E = 1.4426950408889634
COND_ROWS = 16
VMEM_LIMIT = 56 * 1024 * 1024

NT_DIMS = (((1,), (1,)), ((), ()))
TN_DIMS = (((0,), (0,)), ((), ()))


def _sigmoid(x):
    return 1.0 / (1.0 + jnp.exp(-x))


def _silu(x):
    return x * _sigmoid(x)


def _gelu_tanh(x):
    return 0.5 * x * (1.0 + jnp.tanh(math.sqrt(2.0 / math.pi) * (x + 0.044715 * (x * x * x))))


def _rms(x):
    return x * lax.rsqrt(jnp.mean(x * x, axis=-1, keepdims=True) + EPS)


def _split2(x):
    hi = x.astype(BF16)
    lo = (x - hi.astype(F32)).astype(BF16)
    return hi, lo


def _split3(x):
    hi = x.astype(BF16)
    r = x - hi.astype(F32)
    mid = r.astype(BF16)
    lo = (r - mid.astype(F32)).astype(BF16)
    return hi, mid, lo


def _params(sem):
    return pltpu.CompilerParams(dimension_semantics=sem, vmem_limit_bytes=VMEM_LIMIT)


def _mod_kernel(c_ref, w_ref, b_ref, o_ref):
    s = _silu(c_ref[...]).astype(BF16)
    o_ref[...] = jnp.dot(s, w_ref[...].astype(BF16), preferred_element_type=F32) + b_ref[...]


def _modulation(cond, w_mod, b_mod):
    depth, d, n = w_mod.shape
    tn = math.gcd(n, 1024)
    return pl.pallas_call(
        _mod_kernel,
        out_shape=jax.ShapeDtypeStruct((depth, COND_ROWS, n), F32),
        grid=(depth, n // tn),
        in_specs=[
            pl.BlockSpec((COND_ROWS, d), lambda l, j: (0, 0)),
            pl.BlockSpec((None, d, tn), lambda l, j: (l, 0, j)),
            pl.BlockSpec((None, 1, tn), lambda l, j: (l, 0, j)),
        ],
        out_specs=pl.BlockSpec((None, COND_ROWS, tn), lambda l, j: (l, 0, j)),
        compiler_params=_params(("parallel", "parallel")),
        name="adaln_modulation",
    )(cond, w_mod, b_mod.reshape(depth, 1, n))


class _Rows:
    def __init__(self, n_ctx, n_lat_seq, lat_len):
        self.n_ctx, self.lat_len = n_ctx, lat_len
        self.total = n_ctx + n_lat_seq * lat_len

    def cond_row(self, i, tm):
        ctx_tiles = self.n_ctx // tm
        per_seq = self.lat_len // tm
        return jnp.where(i < ctx_tiles, 0, 1 + (i - ctx_tiles) // per_seq)

    def mod_spec(self, k, tm, d):
        return pl.BlockSpec((None, 1, d), lambda i, *_: (self.cond_row(i, tm) * N_MOD + k, 0, 0))


def _in_proj_kernel(*refs, combine):
    if combine:
        x_ref, moe_ref, g2_ref, n_ref, sh_ref, sc_ref, w_ref, o_ref, x2_ref, h_scr = refs
    else:
        x_ref, n_ref, sh_ref, sc_ref, w_ref, o_ref, h_scr = refs

    @pl.when(pl.program_id(1) == 0)
    def _():
        x = x_ref[...]
        if combine:
            x = x + g2_ref[...] * moe_ref[...].astype(F32)
            x2_ref[...] = x
        h = _rms(x) * n_ref[...] * (1.0 + sc_ref[...]) + sh_ref[...]
        h_scr[...] = h.astype(BF16)

    o_ref[...] = jnp.dot(h_scr[...], w_ref[...], preferred_element_type=F32).astype(BF16)


def _in_proj(rows, x, moe, mod_prev, mod_l, norm_g, w_bf, layer, tm, tn):
    t, d = x.shape
    n = w_bf.shape[2]
    combine = moe is not None
    row_spec = pl.BlockSpec((tm, d), lambda i, j: (i, 0))
    row_once_spec = pl.BlockSpec((tm, d), lambda i, j: (i, 0), pipeline_mode=pl.Buffered(1))
    vec_spec = pl.BlockSpec((1, d), lambda i, j: (0, 0))
    in_specs = [row_once_spec]
    args = [x]
    if combine:
        in_specs += [row_once_spec, rows.mod_spec(5, tm, d)]
        args += [moe, mod_prev]
    in_specs += [vec_spec, rows.mod_spec(0, tm, d), rows.mod_spec(1, tm, d),
                 pl.BlockSpec((None, d, tn), lambda i, j: (layer, 0, j))]
    args += [norm_g.reshape(1, d), mod_l, mod_l, w_bf]
    out_shape = [jax.ShapeDtypeStruct((t, n), BF16)]
    out_specs = [pl.BlockSpec((tm, tn), lambda i, j: (i, j))]
    if combine:
        out_shape.append(jax.ShapeDtypeStruct((t, d), F32))
        out_specs.append(row_spec)
    res = pl.pallas_call(
        functools.partial(_in_proj_kernel, combine=combine),
        out_shape=out_shape,
        grid=(t // tm, n // tn),
        in_specs=in_specs,
        out_specs=out_specs,
        scratch_shapes=[pltpu.VMEM((tm, d), BF16)],
        compiler_params=_params(("parallel", "arbitrary")),
        name="in_proj",
    )(*args)
    return (res[0], res[1]) if combine else (res[0], x)


def _rope(x, cos, sin_signed, swap):
    swapped = jnp.dot(x.astype(BF16), swap, preferred_element_type=F32)
    return x * cos + swapped * sin_signed


def _attn_kernel(*refs, latent, past, tq, grp, scale):
    if latent:
        (q_ref, kn_ref, vn_ref, ck_ref, cv_ref, cos_ref, sin_ref, swap_ref, qg_ref, kg_ref,
         o_ref, k_scr, v_scr) = refs
    else:
        q_ref, kn_ref, vn_ref, qg_ref, kg_ref, o_ref, ko_ref, vo_ref, k_scr, v_scr = refs
    qi = pl.program_id(2)

    @pl.when(qi == 0)
    def _():
        k = _rms(kn_ref[...].astype(F32)) * kg_ref[...]
        if latent:
            k = _rope(k, cos_ref[...], sin_ref[...], swap_ref[...])
            k_scr[0:past, :] = ck_ref[...].astype(BF16)
            v_scr[0:past, 0:LANE] = cv_ref[...].astype(BF16)
        else:
            ko_ref[...] = k
            vo_ref[...] = vn_ref[...].astype(F32)
        k_scr[past:, :] = k.astype(BF16)
        v_scr[past:, 0:LANE] = vn_ref[...]
        v_scr[:, LANE:] = jnp.ones((v_scr.shape[0], LANE), BF16)

    q = q_ref[...].astype(F32)
    heads = []
    for h in range(grp):
        qh = _rms(q[:, h * LANE:(h + 1) * LANE]) * qg_ref[...]
        if latent:
            r0 = pl.multiple_of(qi * tq, tq)
            qh = _rope(qh, cos_ref[pl.ds(r0, tq), :], sin_ref[pl.ds(r0, tq), :], swap_ref[...])
        heads.append((qh * scale).astype(BF16))
    qs = jnp.concatenate(heads, axis=0)
    n_keys = k_scr.shape[0]
    chunks = [(c0, min(c0 + ATTN_KEY_CHUNK, n_keys)) for c0 in range(0, n_keys, ATTN_KEY_CHUNK)]

    def scores(c):
        return lax.dot_general(qs, k_scr[c[0]:c[1], :], NT_DIMS, preferred_element_type=F32)

    m = acc = None
    s_next = scores(chunks[0])
    for n, (c0, c1) in enumerate(chunks):
        s = s_next
        if n + 1 < len(chunks):
            s_next = scores(chunks[n + 1])
        m_c = jnp.max(functools.reduce(jnp.maximum, [s[:, t:t + LANE] for t in range(0, c1 - c0, LANE)]),
                      axis=-1, keepdims=True)
        m_new = m_c if m is None else jnp.maximum(m, m_c)
        pv = jnp.dot(jnp.exp2(s - m_new).astype(BF16), v_scr[c0:c1, :], preferred_element_type=F32)
        acc = pv if m is None else acc * jnp.exp2(m - m_new) + pv
        m = m_new
    o = acc[:, 0:LANE] / acc[:, LANE:]
    for h in range(grp):
        o_ref[:, h * LANE:(h + 1) * LANE] = o[h * tq:(h + 1) * tq, :].astype(BF16)


def _attention(p_all, col, row0, n_seq, seq_len, n_heads, kvh, q_gain, k_gain, tq, ctx=None):
    grp = n_heads // kvh
    latent = ctx is not None
    qw = grp * LANE
    assert col["b_q"] % qw == 0 and row0 % seq_len == 0 and seq_len % tq == 0
    seq0, nq = row0 // seq_len, seq_len // tq
    q_spec = pl.BlockSpec((tq, qw), lambda b, g, i: (seq0 * nq + b * nq + i, col["b_q"] // qw + g))
    k_spec = pl.BlockSpec((seq_len, LANE), lambda b, g, i: (seq0 + b, col["b_k"] // LANE + g))
    v_spec = pl.BlockSpec((seq_len, LANE), lambda b, g, i: (seq0 + b, col["b_v"] // LANE + g))
    gain_spec = pl.BlockSpec((1, LANE), lambda b, g, i: (0, 0))
    in_specs = [q_spec, k_spec, v_spec]
    args = [p_all, p_all, p_all]
    past = 0
    if latent:
        cache_k, cache_v, layer, cos, sin, swap = ctx
        past = cache_k.shape[2]
        c_spec = pl.BlockSpec((None, None, past, LANE), lambda b, g, i: (b, layer, 0, g))
        t_spec = pl.BlockSpec((seq_len, LANE), lambda b, g, i: (0, 0))
        in_specs += [c_spec, c_spec, t_spec, t_spec, pl.BlockSpec((LANE, LANE), lambda b, g, i: (0, 0))]
        args += [cache_k, cache_v, cos, sin, swap]
    in_specs += [gain_spec, gain_spec]
    args += [q_gain.reshape(1, LANE), k_gain.reshape(1, LANE)]
    out_shape = [jax.ShapeDtypeStruct((n_seq * seq_len, n_heads * LANE), BF16)]
    out_specs = [pl.BlockSpec((tq, qw), lambda b, g, i: (b * nq + i, g))]
    if not latent:
        kv_shape = jax.ShapeDtypeStruct((n_seq, seq_len, kvh * LANE), F32)
        kv_spec = pl.BlockSpec((None, seq_len, LANE), lambda b, g, i: (b, 0, g))
        out_shape += [kv_shape, kv_shape]
        out_specs += [kv_spec, kv_spec]
    return pl.pallas_call(
        functools.partial(_attn_kernel, latent=latent, past=past, tq=tq, grp=grp, scale=LANE ** -0.5 * LOG2E),
        out_shape=out_shape,
        grid=(n_seq, kvh, nq),
        in_specs=in_specs,
        out_specs=out_specs,
        scratch_shapes=[pltpu.VMEM((past + seq_len, LANE), BF16), pltpu.VMEM((past + seq_len, 2 * LANE), BF16)],
        compiler_params=_params(("parallel", "parallel", "arbitrary")),
        name="attention_latent" if latent else "attention_context",
    )(*args)


def _rope_tables(seq_len):
    quarter = LANE // 4
    inv = ROPE_THETA ** (-jnp.arange(quarter, dtype=F32) / quarter)
    pos = jnp.arange(seq_len)
    row = (pos // GRID_W).astype(F32)[:, None] * inv[None, :]
    colp = (pos % GRID_W).astype(F32)[:, None] * inv[None, :]
    cos = jnp.concatenate([jnp.cos(row), jnp.cos(row), jnp.cos(colp), jnp.cos(colp)], axis=-1)
    sin = jnp.concatenate([-jnp.sin(row), jnp.sin(row), -jnp.sin(colp), jnp.sin(colp)], axis=-1)
    lane = np.arange(LANE)
    partner = np.where((lane // quarter) % 2 == 0, lane + quarter, lane - quarter)
    swap = jnp.asarray(lane[:, None] == partner[None, :], BF16)
    return cos, sin, swap


def _hgrn_level_codes():
    r = np.arange(HGRN_BLOCK)[:, None]
    u = np.arange(HGRN_BLOCK)[None, :]
    diff = r ^ u
    code = np.where(diff == 0, 0, 1 + np.floor(np.log2(np.maximum(diff, 1))).astype(np.int32))
    return [jnp.asarray(np.where(keep, code, -1), jnp.int32) for keep in (r >= u, r <= u)]


def _segment_runs(x):
    n_tiles = x.shape[0] // SUBLANES
    sub = lax.broadcasted_iota(jnp.int32, (SUBLANES, x.shape[1]), 0)
    pre_t, suf_t = [], []
    for i in range(n_tiles):
        p = s = x[i * SUBLANES:(i + 1) * SUBLANES]
        pre_i, suf_i = [], []
        n = 1
        while n < SUBLANES:
            pos = sub % (2 * n)
            add_p, add_s = jnp.zeros_like(p), jnp.zeros_like(s)
            for j in range(n):
                add_p = jnp.where(pos == n + j, pltpu.roll(p, j + 1, 0), add_p)
                add_s = jnp.where(pos == n - 1 - j, pltpu.roll(s, SUBLANES - (j + 1), 0), add_s)
            p, s = p + add_p, s + add_s
            pre_i.append(p)
            suf_i.append(s)
            n *= 2
        pre_t.append(pre_i)
        suf_t.append(suf_i)
    levels = len(pre_t[0])
    pre = [[t[m] for t in pre_t] for m in range(levels)]
    suf = [[t[m] for t in suf_t] for m in range(levels)]
    cur_p, cur_s = pre[-1], suf[-1]
    tot = [jnp.broadcast_to(t[SUBLANES - 1:SUBLANES, :], t.shape) for t in cur_p]
    seg = 1
    while seg < n_tiles:
        new_p, new_s, new_tot = list(cur_p), list(cur_s), list(tot)
        for base in range(0, n_tiles, 2 * seg):
            t_first, t_second = tot[base], tot[base + seg]
            both = t_first + t_second
            for i in range(base, base + seg):
                new_s[i] = cur_s[i] + t_second
                new_p[i + seg] = cur_p[i + seg] + t_first
                new_tot[i] = new_tot[i + seg] = both
        cur_p, cur_s, tot = new_p, new_s, new_tot
        pre.append(cur_p)
        suf.append(cur_s)
        seg *= 2
    join = lambda tiles: jnp.concatenate(tiles, axis=0)
    return [join(t) for t in pre], [join(t) for t in suf]


def _hgrn_kernel(q_ref, ff_ref, fb_ref, i_ref, g_ref, lb_ref, ng_ref, s0_ref, lvlf_ref, lvlb_ref, y_ref,
                 sfin_ref, of_scr, ob_scr, *, seq_len):
    blk, n_lvl = HGRN_BLOCK, HGRN_LEVELS
    n_blk = seq_len // blk

    def scan_blocks(chains):
        q, k, v, q_run, k_run = [], [], [], [], []
        for r0, f_ref, lb, _, forward, _, _ in chains:
            q.append(_silu(q_ref[pl.ds(r0, blk), :].astype(F32)))
            g = lb + (1.0 - lb) * _sigmoid(f_ref[pl.ds(r0, blk), :].astype(F32))
            k.append(1.0 - g)
            v.append(i_ref[pl.ds(r0, blk), :])
            log_g = jnp.log(g) * LOG2E
            pre, suf = _segment_runs(log_g)
            near, far = (pre, suf) if forward else (suf, pre)
            q_run.append([log_g] + near)
            k_run.append([None] + [x - log_g for x in far])
        kb = [x.astype(BF16) for x in k]
        level = [c[5][...] for c in chains]
        score = [jnp.where(lv == 0, lax.dot_general(qc.astype(BF16), kc, NT_DIMS, preferred_element_type=F32), 0.0)
                 for lv, qc, kc in zip(level, q, kb)]
        for m in range(n_lvl):
            for i in range(len(chains)):
                qs = (q[i] * jnp.exp2(q_run[i][m])).astype(BF16)
                ks = kb[i] if m == 0 else (k[i] * jnp.exp2(k_run[i][m])).astype(BF16)
                score[i] = jnp.where(level[i] == m + 1,
                                     lax.dot_general(qs, ks, NT_DIMS, preferred_element_type=F32), score[i])
        out = []
        for i, (_, _, _, st, _, _, edge) in enumerate(chains):
            if isinstance(st, int):
                st = out[st][1]
            q_all = jnp.exp2(q_run[i][n_lvl])
            o = jnp.dot(score[i].astype(BF16), v[i], preferred_element_type=F32)
            o += lax.dot_general((q[i] * q_all).astype(BF16), st.astype(BF16), NT_DIMS,
                                 preferred_element_type=F32)
            kd = (k[i] * jnp.exp2(k_run[i][n_lvl])).astype(BF16)
            st = st * q_all[edge:edge + 1, :] + lax.dot_general(v[i], kd, TN_DIMS, preferred_element_type=F32)
            out.append((o, st))
        return out

    lb_f, lb_b = lb_ref[0:1, :], lb_ref[1:2, :]

    def body(j, carry):
        st_f, st_b = carry
        r_f = pl.multiple_of(2 * j * blk, 2 * blk)
        r_b = pl.multiple_of((n_blk - 2 - 2 * j) * blk, 2 * blk)
        res = scan_blocks([(r_f, ff_ref, lb_f, st_f, True, lvlf_ref, blk - 1),
                           (r_b + blk, fb_ref, lb_b, st_b, False, lvlb_ref, 0),
                           (r_f + blk, ff_ref, lb_f, 0, True, lvlf_ref, blk - 1),
                           (r_b, fb_ref, lb_b, 1, False, lvlb_ref, 0)])
        of_scr[pl.ds(r_f, blk), :] = res[0][0]
        of_scr[pl.ds(r_f + blk, blk), :] = res[2][0]
        ob_scr[pl.ds(r_b + blk, blk), :] = res[1][0]
        ob_scr[pl.ds(r_b, blk), :] = res[3][0]
        return res[2][1], res[3][1]

    st_f, st_b = lax.fori_loop(0, n_blk // 2, body, (s0_ref[0].T, s0_ref[1].T))
    sfin_ref[0] = st_f.T
    sfin_ref[1] = st_b.T
    o = _rms(of_scr[...] + ob_scr[...]) * ng_ref[...]
    y_ref[...] = (o * _silu(g_ref[...].astype(F32))).astype(BF16)


def _hgrn(p_all, col, row0, n_seq, seq_len, n_heads, lb_l, norm_g, s0):
    assert row0 % seq_len == 0 and seq_len % (2 * HGRN_BLOCK) == 0
    seq0 = row0 // seq_len

    def cspec(name):
        return pl.BlockSpec((seq_len, LANE), lambda b, h: (seq0 + b, col[name] // LANE + h))

    st_spec = pl.BlockSpec((None, 2, None, LANE, LANE), lambda b, h: (b, 0, h, 0, 0))
    tables = _hgrn_level_codes()
    return pl.pallas_call(
        functools.partial(_hgrn_kernel, seq_len=seq_len),
        out_shape=[jax.ShapeDtypeStruct((n_seq * seq_len, n_heads * LANE), BF16),
                   jax.ShapeDtypeStruct(s0.shape, F32)],
        grid=(n_seq, n_heads),
        in_specs=[cspec("a_q"), cspec("a_ff"), cspec("a_fb"), cspec("a_i"), cspec("a_g"),
                  pl.BlockSpec((2, LANE), lambda b, h: (0, h)),
                  pl.BlockSpec((1, LANE), lambda b, h: (0, 0)),
                  st_spec] + [pl.BlockSpec(t.shape, lambda b, h: (0, 0)) for t in tables],
        out_specs=[pl.BlockSpec((seq_len, LANE), lambda b, h: (b, h)), st_spec],
        scratch_shapes=[pltpu.VMEM((seq_len, LANE), F32), pltpu.VMEM((seq_len, LANE), F32)],
        compiler_params=_params(("parallel", "parallel")),
        name="hgrn_%d" % seq_len,
    )(p_all, p_all, p_all, p_all, p_all, lb_l, norm_g.reshape(1, LANE), s0, *tables)


def _gmlp_kernel(u_ref, v_ref, gn_ref, ws_ref, bs_ref, o_ref, *, n_chunks, groups):
    u = _gelu_tanh(u_ref[...].astype(F32))
    v = (_rms(_gelu_tanh(v_ref[...].astype(F32))) * gn_ref[...]).astype(BF16)
    for n in range(n_chunks):
        rs = slice(n * MIX_CHUNK, (n + 1) * MIX_CHUNK)
        for g in range(groups):
            cs = slice(g * LANE, (g + 1) * LANE)
            mixed = jnp.dot(ws_ref[g], v[rs, cs], preferred_element_type=F32) + bs_ref[g]
            o_ref[rs, cs] = (u[rs, cs] * mixed).astype(BF16)


def _gmlp(p_all, col, norm_g, ws_bf, bias, tm):
    t = p_all.shape[0]
    groups = ws_bf.shape[0]
    cw = groups * LANE
    assert col["c_u"] % cw == 0 and col["c_v"] % cw == 0
    return pl.pallas_call(
        functools.partial(_gmlp_kernel, n_chunks=tm // MIX_CHUNK, groups=groups),
        out_shape=jax.ShapeDtypeStruct((t, cw), BF16),
        grid=(t // tm,),
        in_specs=[pl.BlockSpec((tm, cw), lambda i: (i, col["c_u"] // cw)),
                  pl.BlockSpec((tm, cw), lambda i: (i, col["c_v"] // cw)),
                  pl.BlockSpec((1, cw), lambda i: (0, 0)),
                  pl.BlockSpec((groups, MIX_CHUNK, MIX_CHUNK), lambda i: (0, 0, 0)),
                  pl.BlockSpec((groups, MIX_CHUNK, LANE), lambda i: (0, 0, 0))],
        out_specs=pl.BlockSpec((tm, cw), lambda i: (i, 0)),
        compiler_params=_params(("parallel",)),
        name="gmlp",
    )(p_all, p_all, norm_g.reshape(1, cw), ws_bf, bias)


def _route_rows(logit_rows):
    per = len(logit_rows) // N_GROUPS
    m = functools.reduce(jnp.maximum, logit_rows)
    p = [jnp.exp(r - m) for r in logit_rows]

    def top2_sum(a):
        pairs = [a[i] + a[j] for i in range(len(a)) for j in range(i + 1, len(a))]
        return functools.reduce(jnp.maximum, pairs)

    scores = [top2_sum(p[g * per:(g + 1) * per]) for g in range(N_GROUPS)]
    best, gi = scores[0], jnp.zeros(m.shape, jnp.int32)
    for g in range(1, N_GROUPS):
        better = scores[g] > best
        gi = jnp.where(better, g, gi)
        best = jnp.where(better, scores[g], best)
    a = []
    for j in range(per):
        aj = p[(N_GROUPS - 1) * per + j]
        for g in reversed(range(N_GROUPS - 1)):
            aj = jnp.where(gi == g, p[g * per + j], aj)
        a.append(aj)
    v1, i1 = a[0], jnp.zeros(m.shape, jnp.int32)
    for j in range(1, per):
        better = a[j] > v1
        i1 = jnp.where(better, j, i1)
        v1 = jnp.where(better, a[j], v1)
    v2, i2 = jnp.full(m.shape, -1.0, F32), jnp.zeros(m.shape, jnp.int32)
    for j in range(per):
        better = (i1 != j) & (a[j] > v2)
        i2 = jnp.where(better, j, i2)
        v2 = jnp.where(better, a[j], v2)
    inv = 1.0 / (v1 + v2)
    return gi * per + i1, gi * per + i2, v1 * inv, v2 * inv


def _merge_kernel(x_ref, ya_ctx_ref, ya_lat_ref, yb_ctx_ref, yb_lat_ref, yc_ref, ga0, ga1, gb0, gb1, gc0, gc1,
                  wpa_ref, wpb_ref, wpc_ref, wout_ref, g1_ref, n2_ref, sh2_ref, sc2_ref, rpair_ref, rb_ref,
                  x1_ref, h2_ref, rt_ref, cnt_ref, *, n_experts, ctx_tiles):
    def gate(r0, r1):
        return jnp.concatenate([_sigmoid(r0[...].astype(F32)), _sigmoid(r1[...].astype(F32))], axis=-1)

    is_ctx = pl.program_id(0) < ctx_tiles
    ya = jnp.where(is_ctx, ya_ctx_ref[...], ya_lat_ref[...])
    yb = jnp.where(is_ctx, yb_ctx_ref[...], yb_lat_ref[...])
    merged = gate(ga0, ga1) * jnp.dot(ya, wpa_ref[...], preferred_element_type=F32)
    merged += gate(gb0, gb1) * jnp.dot(yb, wpb_ref[...], preferred_element_type=F32)
    merged += gate(gc0, gc1) * jnp.dot(yc_ref[...], wpc_ref[...], preferred_element_type=F32)
    out = jnp.dot(merged.astype(BF16), wout_ref[...], preferred_element_type=F32)
    x1 = x_ref[...] + g1_ref[...] * out
    x1_ref[...] = x1
    h2 = _rms(x1) * n2_ref[...] * (1.0 + sc2_ref[...]) + sh2_ref[...]
    h2_ref[...] = h2.astype(BF16)
    hi, lo = _split2(h2)
    r_pair = rpair_ref[...]
    parts = (jnp.dot(hi, r_pair, preferred_element_type=F32) + jnp.dot(lo, r_pair, preferred_element_type=F32)).T
    lt = parts[0:n_experts, :] + parts[n_experts:2 * n_experts, :] + rb_ref[...]
    tm = lt.shape[1]
    e1, e2, w1, w2 = _route_rows([lt[e:e + 1, :] for e in range(n_experts)])

    @pl.when(pl.program_id(0) == 0)
    def _():
        cnt_ref[...] = jnp.zeros_like(cnt_ref)

    e_id = lax.broadcasted_iota(jnp.int32, (n_experts, tm), 0)
    hit1, hit2 = e_id == e1, e_id == e2
    onehot = (hit1 | hit2).astype(BF16)
    src = lax.broadcasted_iota(jnp.int32, (tm, tm), 0)
    dst = lax.broadcasted_iota(jnp.int32, (tm, tm), 1)
    before = jnp.dot(onehot, (src < dst).astype(BF16), preferred_element_type=F32) + cnt_ref[...]
    rank1 = jnp.sum(jnp.where(hit1, before, 0.0), axis=0, keepdims=True)
    rank2 = jnp.sum(jnp.where(hit2, before, 0.0), axis=0, keepdims=True)
    cnt_ref[...] += jnp.dot(onehot, jnp.ones((tm, tm), BF16), preferred_element_type=F32)
    zero = jnp.zeros_like(w1)
    rt_ref[...] = jnp.concatenate([e1.astype(F32), e2.astype(F32), w1, w2, rank1, rank2, zero, zero], axis=0)


ROUTE_ROWS = 8


def _merge(rows, x, y_a, y_b, y_c, p_all, col, mod_l, norm2_g, w_pa, w_pb, w_pc, w_out, layer, r_pair,
           r_bias, tm):
    t, d = x.shape
    n_experts = r_bias.shape[0]
    gw = d // 2
    assert col["gates"] % gw == 0
    g0 = col["gates"] // gw
    ctx_tiles = rows.n_ctx // tm
    row = lambda w: pl.BlockSpec((tm, w), lambda i: (i, 0))
    ctx_row = lambda w: pl.BlockSpec((tm, w), lambda i: (jnp.minimum(i, ctx_tiles - 1), 0))
    lat_row = lambda w: pl.BlockSpec((tm, w), lambda i: (jnp.maximum(i - ctx_tiles, 0), 0))
    full = lambda a: pl.BlockSpec(a.shape, lambda i: (0,) * a.ndim, pipeline_mode=pl.Buffered(1))
    of_layer = lambda a: pl.BlockSpec((None,) + a.shape[1:], lambda i: (layer,) + (0,) * (a.ndim - 1),
                                      pipeline_mode=pl.Buffered(1))
    gate_specs = [pl.BlockSpec((tm, gw), functools.partial(lambda i, k: (i, g0 + k), k=k)) for k in range(6)]
    return pl.pallas_call(
        functools.partial(_merge_kernel, n_experts=n_experts, ctx_tiles=ctx_tiles),
        out_shape=[jax.ShapeDtypeStruct((t, d), F32), jax.ShapeDtypeStruct((t, d), BF16),
                   jax.ShapeDtypeStruct((ROUTE_ROWS, t), F32), jax.ShapeDtypeStruct((n_experts, tm), F32)],
        grid=(t // tm,),
        in_specs=[row(d), ctx_row(y_a[0].shape[1]), lat_row(y_a[1].shape[1]), ctx_row(y_b[0].shape[1]),
                  lat_row(y_b[1].shape[1]), row(y_c.shape[1])] + gate_specs
                 + [of_layer(w_pa), of_layer(w_pb), of_layer(w_pc), of_layer(w_out), rows.mod_spec(2, tm, d),
                    pl.BlockSpec((1, d), lambda i: (0, 0)), rows.mod_spec(3, tm, d), rows.mod_spec(4, tm, d),
                    full(r_pair), pl.BlockSpec((n_experts, tm), lambda i: (0, 0))],
        out_specs=[row(d), row(d), pl.BlockSpec((ROUTE_ROWS, tm), lambda i: (0, i)),
                   pl.BlockSpec((n_experts, tm), lambda i: (0, 0))],
        compiler_params=_params(("arbitrary",)),
        name="merge_out_proj",
    )(x, *y_a, *y_b, y_c, *([p_all] * 6), w_pa, w_pb, w_pc, w_out, mod_l, norm2_g.reshape(1, d), mod_l, mod_l,
      r_pair, jnp.broadcast_to(r_bias.astype(F32)[:, None], (n_experts, tm)))


def _moe_kernel(be_ref, bv_ref, x_ref, wg_ref, wu_ref, wd_ref, *rest):
    o_ref = rest[-1]
    i = pl.program_id(0)

    @pl.when(bv_ref[i] > 0)
    def _():
        x = x_ref[...]
        a = jnp.dot(x, wg_ref[...], preferred_element_type=F32)
        b = jnp.dot(x, wu_ref[...], preferred_element_type=F32)
        h = (_silu(a) * b).astype(BF16)
        o_ref[...] = jnp.dot(h, wd_ref[...], preferred_element_type=F32).astype(BF16)

    @pl.when(bv_ref[i] == 0)
    def _():
        o_ref[...] = jnp.zeros_like(o_ref)


def _moe_ffn(xb, block_e, block_valid, wg, wu, wd, layer, tm, n_slots, block0=0, out_prev=None):
    n_rows, d = xb.shape
    ff = wg.shape[3]
    in_specs = [pl.BlockSpec((tm, d), lambda i, be, bv: (i, 0)),
                pl.BlockSpec((None, None, d, ff), lambda i, be, bv: (layer, be[i], 0, 0)),
                pl.BlockSpec((None, None, d, ff), lambda i, be, bv: (layer, be[i], 0, 0)),
                pl.BlockSpec((None, None, ff, d), lambda i, be, bv: (layer, be[i], 0, 0))]
    args = [block_e, block_valid, xb, wg, wu, wd]
    aliases = {}
    if out_prev is not None:
        in_specs.append(pl.BlockSpec(memory_space=pl.ANY))
        aliases = {len(args): 0}
        args.append(out_prev)
    grid_spec = pltpu.PrefetchScalarGridSpec(
        num_scalar_prefetch=2,
        grid=(n_rows // tm,),
        in_specs=in_specs,
        out_specs=pl.BlockSpec((tm, d), lambda i, be, bv: (block0 + i, 0)),
    )
    return pl.pallas_call(
        _moe_kernel,
        out_shape=jax.ShapeDtypeStruct((n_slots, d), BF16),
        grid_spec=grid_spec,
        input_output_aliases=aliases,
        compiler_params=_params(("arbitrary",)),
        name="moe_experts",
    )(*args)


def _dispatch(expert, rank, counts, tm):
    n_experts = counts.shape[0]
    n_tok = expert.shape[1]
    n_assign = TOP_K * n_tok
    padded = (counts + tm - 1) // tm * tm
    pad_end = jnp.cumsum(padded)
    pad_start = pad_end - padded
    first_slot = jnp.sum(jnp.where(expert[..., None] == jnp.arange(n_experts), pad_start, 0), axis=-1)
    dest = first_slot + rank
    n_blocks = -(-(n_assign + n_experts * (tm - 1)) // tm)
    tok = jnp.broadcast_to(jnp.arange(n_tok, dtype=jnp.int32)[None, :], dest.shape)
    slot_tok = jnp.zeros((n_blocks * tm,), jnp.int32).at[dest.reshape(-1)].set(
        tok.reshape(-1), unique_indices=True, mode="promise_in_bounds")
    block_start = jnp.arange(n_blocks, dtype=jnp.int32) * tm
    block_e = jnp.minimum(jnp.sum(block_start[:, None] >= pad_end[None, :], axis=1), n_experts - 1)
    block_valid = (block_start < pad_end[-1]).astype(jnp.int32)
    return slot_tok, dest, block_e.astype(jnp.int32), block_valid


def _final_kernel(x_ref, moe_ref, g2_ref, nf_ref, o_ref):
    x = x_ref[...] + g2_ref[...] * moe_ref[...].astype(F32)
    o_ref[...] = _rms(x) * nf_ref[...]


def _final_norm(rows, x, moe, mod_l, norm_f, row0, n_rows, tm):
    d = x.shape[1]
    t0 = row0 // tm
    row_spec = pl.BlockSpec((tm, d), lambda i: (t0 + i, 0))
    g2_spec = pl.BlockSpec((None, 1, d), lambda i: (rows.cond_row(t0 + i, tm) * N_MOD + 5, 0, 0))
    return pl.pallas_call(
        _final_kernel,
        out_shape=jax.ShapeDtypeStruct((n_rows, d), F32),
        grid=(n_rows // tm,),
        in_specs=[row_spec, row_spec, g2_spec, pl.BlockSpec((1, d), lambda i: (0, 0))],
        out_specs=pl.BlockSpec((tm, d), lambda i: (i, 0)),
        compiler_params=_params(("parallel",)),
        name="final_norm",
    )(x, moe, mod_l, norm_f.reshape(1, d))


def kernel(x_prompt, x_sample, cache_k, cache_v, state_hgrn, c, c_ctx, w_mod, b_mod, norm1, norm2, norm_f,
           w_in, hgrn_lb, hgrn_norm, q_norm, k_norm, gmlp_norm, gmlp_ws, gmlp_bs, w_pa, w_pb, w_pc, w_out,
           router_w, router_b, exp_gate, exp_up, exp_down):
    n_ctx_seq, ctx_len, d = x_prompt.shape
    n_lat_seq, lat_len, _ = x_sample.shape
    depth = w_mod.shape[0]
    past, kvh = cache_k.shape[2], cache_k.shape[3]
    a_heads = state_hgrn.shape[3]
    a_width, b_width, c_width = hgrn_lb.shape[2], w_pb.shape[1], w_pc.shape[1]
    b_heads = b_width // LANE
    n_experts = router_w.shape[1]
    assert cache_k.shape[4] == LANE and state_hgrn.shape[4:] == (LANE, LANE) and a_width == a_heads * LANE

    sizes = [("a_q", a_width), ("a_ff", a_width), ("a_fb", a_width), ("a_i", a_width), ("a_g", a_width),
             ("b_q", b_width), ("b_k", kvh * LANE), ("b_v", kvh * LANE), ("c_u", c_width), ("c_v", c_width),
             ("gates", 3 * d)]
    col, acc = {}, 0
    for name, width in sizes:
        col[name] = acc
        acc += width
    assert acc == w_in.shape[2]

    n_ctx = n_ctx_seq * ctx_len
    rows = _Rows(n_ctx, n_lat_seq, lat_len)
    tm_proj = min(1024, n_ctx, lat_len)
    tn_proj = min(1024, math.gcd(w_in.shape[2], 1024))
    tm_row = min(256, n_ctx, lat_len)
    tm_moe = 256
    tq_lat = min(256, lat_len)

    cond = jnp.concatenate([c_ctx[None, :], c, jnp.zeros((COND_ROWS - 1 - n_lat_seq, d), F32)], axis=0)
    mod = _modulation(cond, w_mod, b_mod).reshape(depth, COND_ROWS * N_MOD, 1, d)

    p_lb = jax.nn.softmax(hgrn_lb.astype(F32), axis=1)
    lbs = jnp.cumsum(p_lb, axis=1) - p_lb[:, :1]

    cos, sin, swap = _rope_tables(lat_len)
    cache_k2 = cache_k.reshape(n_lat_seq, depth, past, kvh * LANE)
    cache_v2 = cache_v.reshape(n_lat_seq, depth, past, kvh * LANE)
    s0_ctx = jnp.zeros((n_ctx_seq,) + state_hgrn.shape[2:], F32)
    assert 2 * n_experts <= LANE and n_experts % SUBLANES == 0
    r_hi, r_lo = _split2(router_w.astype(F32))
    r_pair = jnp.zeros((d, LANE), BF16).at[:, :n_experts].set(r_hi).at[:, n_experts:2 * n_experts].set(r_lo)
    gmlp_bias = jnp.broadcast_to(gmlp_bs[:, :, :, None], gmlp_bs.shape + (LANE,))

    w_in_bf, w_pa_bf, w_pb_bf, w_pc_bf, w_out_bf, gate_bf, up_bf, down_bf = (
        w.astype(BF16) for w in (w_in, w_pa, w_pb, w_pc, w_out, exp_gate, exp_up, exp_down))

    x = jnp.concatenate([x_prompt.reshape(n_ctx, d), x_sample.reshape(n_lat_seq * lat_len, d)], axis=0)
    moe = None
    new_k, new_v, new_s = [], [], []
    for l in range(depth):
        mod_l = mod[l]
        p_all, x = _in_proj(rows, x, moe, mod[l - 1] if l else None, mod_l, norm1[l], w_in_bf, l,
                            tm_proj, tn_proj)
        yb_ctx, k_l, v_l = _attention(p_all, col, 0, n_ctx_seq, ctx_len, b_heads, kvh, q_norm[l], k_norm[l],
                                      ctx_len)
        (yb_lat,) = _attention(p_all, col, n_ctx, n_lat_seq, lat_len, b_heads, kvh, q_norm[l], k_norm[l],
                               tq_lat, ctx=(cache_k2, cache_v2, l, cos, sin, swap))
        ya_ctx, s_l = _hgrn(p_all, col, 0, n_ctx_seq, ctx_len, a_heads, lbs[:, l], hgrn_norm[l], s0_ctx)
        ya_lat, _ = _hgrn(p_all, col, n_ctx, n_lat_seq, lat_len, a_heads, lbs[:, l], hgrn_norm[l],
                          state_hgrn[:, l])
        y_c = _gmlp(p_all, col, gmlp_norm[l], gmlp_ws[l].astype(BF16), gmlp_bias[l], tm_proj)
        x, h2, route, counts = _merge(rows, x, (ya_ctx, ya_lat), (yb_ctx, yb_lat), y_c, p_all, col, mod_l,
                                      norm2[l], w_pa_bf, w_pb_bf, w_pc_bf, w_out_bf, l, r_pair, router_b,
                                      tm_row)
        slot_tok, dest, block_e, block_valid = _dispatch(
            route[0:2].astype(jnp.int32), route[4:6].astype(jnp.int32), counts[:, 0].astype(jnp.int32), tm_moe)
        rows_of = lambda a, idx: a.at[idx].get(mode="promise_in_bounds")
        n_slots = slot_tok.shape[0]
        cuts = [n_slots // tm_moe * r // MOE_RANGES for r in range(MOE_RANGES + 1)]
        yb = None
        for b0, b1 in zip(cuts[:-1], cuts[1:]):
            yb = _moe_ffn(rows_of(h2, slot_tok[b0 * tm_moe:b1 * tm_moe]), block_e[b0:b1], block_valid[b0:b1],
                          gate_bf, up_bf, down_bf, l, tm_moe, n_slots, block0=b0, out_prev=yb)
        moe = (rows_of(yb, dest[0]).astype(F32) * route[2][:, None]
               + rows_of(yb, dest[1]).astype(F32) * route[3][:, None]).astype(BF16)
        new_k.append(k_l)
        new_v.append(v_l)
        new_s.append(s_l)

    y_ctx = _final_norm(rows, x, moe, mod[depth - 1], norm_f, 0, n_ctx, tm_row)
    y_lat = _final_norm(rows, x, moe, mod[depth - 1], norm_f, n_ctx, n_lat_seq * lat_len, tm_row)
    kv_shape = (n_ctx_seq, depth, ctx_len, kvh, LANE)
    return (y_ctx.reshape(n_ctx_seq, ctx_len, d), y_lat.reshape(n_lat_seq, lat_len, d),
            jnp.stack(new_k, axis=1).reshape(kv_shape), jnp.stack(new_v, axis=1).reshape(kv_shape),
            jnp.stack(new_s, axis=1))
```
